```python
import jax, jax.numpy as jnp
from jax import lax
import numpy as np

D_MODEL = 2048
BATCH = 8
SEQ = 4096
DEPTH = 1
DEC_BATCH = 16
DEC_SEQ = 32
PAST_LEN = 1024

CHUNK = 64
PLE_DIM = 256
N_Q_HEADS = 16
N_KV_HEADS = 4
GROUP = N_Q_HEADS // N_KV_HEADS
HEAD_DIM = 64
WINDOW = 128
WINDOW_CHUNKS = WINDOW // CHUNK
SWA_ROWS = WINDOW_CHUNKS * CHUNK
BAND = (WINDOW_CHUNKS + 1) * CHUNK
Q_A_DIM = N_Q_HEADS * HEAD_DIM
KV_A_DIM = N_KV_HEADS * HEAD_DIM
N_RET_HEADS = 8
RET_DK = 128
RET_DV = 256
RET_QK_DIM = N_RET_HEADS * RET_DK
RET_V_DIM = N_RET_HEADS * RET_DV
D_FF = ((8 * D_MODEL // 3 + 255) // 256) * 256
SPLIT_WIDTHS = (Q_A_DIM, KV_A_DIM, KV_A_DIM, RET_QK_DIM, RET_QK_DIM, RET_V_DIM, RET_V_DIM, D_MODEL, D_MODEL)
D_IN = sum(SPLIT_WIDTHS)
NEG_INF = -1e30
EPS = 1e-6

kernel_name = 'sandwich_swa_sink_retention_hybrid_step'


def rms_norm(x, g):
    xf = x.astype(jnp.float32)
    y = xf * lax.rsqrt(jnp.mean(xf * xf, axis=-1, keepdims=True) + EPS)
    return (y * g.astype(jnp.float32)).astype(x.dtype)


def alibi_slopes():
    h = jnp.arange(N_Q_HEADS, dtype=jnp.float32)
    return jnp.exp2(-8.0 * (h + 1.0) / N_Q_HEADS)


def retention_log_decay():
    h = jnp.arange(N_RET_HEADS, dtype=jnp.float32)
    return jnp.log(1.0 - jnp.exp2(-5.0 - h))


def sink_attention(q, k, v, sinks, dist, valid):
    s = jnp.einsum('...qhgd,...khd->...hgqk', q, k).astype(jnp.float32) * (HEAD_DIM ** -0.5)
    s = s - alibi_slopes().reshape(N_KV_HEADS, GROUP, 1, 1) * dist.astype(jnp.float32)
    if valid is not None:
        s = jnp.where(valid, s, NEG_INF)
    sink = sinks.astype(jnp.float32).reshape(N_KV_HEADS, GROUP, 1, 1)
    m = jnp.maximum(jnp.max(s, axis=-1, keepdims=True), sink)
    e = jnp.exp(s - m)
    p = e / (jnp.sum(e, axis=-1, keepdims=True) + jnp.exp(sink - m))
    return jnp.einsum('...hgqk,...khd->...qhgd', p.astype(v.dtype), v)


def swa_prompt(q, k, v, sinks):
    b, s_len = q.shape[0], q.shape[1]
    n_chunks = s_len // CHUNK
    qb = q.reshape(b, n_chunks, CHUNK, N_KV_HEADS, GROUP, HEAD_DIM)
    pad = ((0, 0), (WINDOW_CHUNKS, 0), (0, 0), (0, 0), (0, 0))
    kp = jnp.pad(k.reshape(b, n_chunks, CHUNK, N_KV_HEADS, HEAD_DIM), pad)
    vp = jnp.pad(v.reshape(b, n_chunks, CHUNK, N_KV_HEADS, HEAD_DIM), pad)
    kb = jnp.concatenate([kp[:, j:j + n_chunks] for j in range(WINDOW_CHUNKS + 1)], axis=2)
    vb = jnp.concatenate([vp[:, j:j + n_chunks] for j in range(WINDOW_CHUNKS + 1)], axis=2)
    key_chunk = jnp.arange(n_chunks)[:, None] + jnp.arange(BAND)[None, :] // CHUNK - WINDOW_CHUNKS
    valid = (key_chunk >= 0)[:, None, None, None, :]
    qpos = jnp.arange(CHUNK) + SWA_ROWS
    kpos = jnp.arange(BAND)
    dist = jnp.abs(qpos[:, None] - kpos[None, :])
    out = sink_attention(qb, kb, vb, sinks, dist, valid)
    return out.reshape(b, s_len, Q_A_DIM)


def swa_step(q, k, v, sinks, cache_k, cache_v):
    b, L = q.shape[0], q.shape[1]
    kk = jnp.concatenate([cache_k.astype(k.dtype), k], axis=1)
    vv = jnp.concatenate([cache_v.astype(v.dtype), v], axis=1)
    qpos = jnp.arange(L) + SWA_ROWS
    kpos = jnp.arange(SWA_ROWS + L)
    dist = jnp.abs(qpos[:, None] - kpos[None, :])
    out = sink_attention(q, kk, vv, sinks, dist, None)
    return out.reshape(b, L, Q_A_DIM), kk[:, L:], vv[:, L:]


def retention_block(q, k, v, state, log_gamma):
    L = q.shape[1]
    idx = jnp.arange(L, dtype=jnp.float32)
    diff = idx[:, None] - idx[None, :]
    decay = jnp.where(diff >= 0, jnp.exp(log_gamma[:, None, None] * jnp.maximum(diff, 0.0)), 0.0)
    scores = jnp.einsum('bihd,bjhd->bhij', q, k) * decay
    inner = jnp.einsum('bhij,bjhv->bihv', scores, v)
    cross_decay = jnp.exp(log_gamma[None, :] * (idx[:, None] + 1.0))
    cross = jnp.einsum('bihd,bhdv->bihv', q, state) * cross_decay[None, :, :, None]
    k_decay = jnp.exp(log_gamma[None, :] * (L - 1.0 - idx[:, None]))
    new_state = (state * jnp.exp(log_gamma * L)[None, :, None, None]
                 + jnp.einsum('bjhd,bjhv->bhdv', k * k_decay[None, :, :, None], v))
    return inner + cross, new_state


def retention_prompt(q, k, v, log_gamma):
    b, s_len = q.shape[0], q.shape[1]
    n_chunks = s_len // CHUNK

    def to_chunks(t):
        return jnp.moveaxis(t.reshape(b, n_chunks, CHUNK, t.shape[2], t.shape[3]), 1, 0)

    def step(state, xs):
        qc, kc, vc = xs
        o, state = retention_block(qc, kc, vc, state, log_gamma)
        return state, o

    s0 = jnp.zeros((b, N_RET_HEADS, RET_DK, RET_DV), jnp.float32)
    s_fin, o = lax.scan(step, s0, (to_chunks(q), to_chunks(k), to_chunks(v)))
    return jnp.moveaxis(o, 0, 1).reshape(b, s_len, N_RET_HEADS, RET_DV), s_fin


def head_group_norm(o):
    mu = jnp.mean(o, axis=-1, keepdims=True)
    c = o - mu
    return c * lax.rsqrt(jnp.mean(c * c, axis=-1, keepdims=True) + EPS)


def layer_forward(x, ple, norm_pre_mix, w_in, attn_sinks, w_a_up, w_r_up, w_out,
                  norm_post_mix, norm_pre_ffn, w_ffn_in, w_ffn_out, norm_post_ffn,
                  norm_ple, w_ple_gate, w_ple, cache_k=None, cache_v=None, ret_state=None):
    b, L, _ = x.shape
    u = rms_norm(x, norm_pre_mix)
    proj = u @ w_in
    q_a, k_a, v_a, q_r, k_r, v_r, g_r, gate_a, gate_b = jnp.split(
        proj, [int(i) for i in np.cumsum(SPLIT_WIDTHS)[:-1]], axis=-1)
    q_a = q_a.reshape(b, L, N_KV_HEADS, GROUP, HEAD_DIM)
    k_a = k_a.reshape(b, L, N_KV_HEADS, HEAD_DIM)
    v_a = v_a.reshape(b, L, N_KV_HEADS, HEAD_DIM)
    if cache_k is None:
        a_out = swa_prompt(q_a, k_a, v_a, attn_sinks)
        new_k, new_v = k_a[:, L - SWA_ROWS:], v_a[:, L - SWA_ROWS:]
    else:
        a_out, new_k, new_v = swa_step(q_a, k_a, v_a, attn_sinks, cache_k, cache_v)
    log_gamma = retention_log_decay()
    qr = q_r.reshape(b, L, N_RET_HEADS, RET_DK).astype(jnp.float32)
    kr = k_r.reshape(b, L, N_RET_HEADS, RET_DK).astype(jnp.float32) * (RET_DK ** -0.5)
    vr = v_r.reshape(b, L, N_RET_HEADS, RET_DV).astype(jnp.float32)
    if ret_state is None:
        r_out, new_state = retention_prompt(qr, kr, vr, log_gamma)
    else:
        r_out, new_state = retention_block(qr, kr, vr, ret_state.astype(jnp.float32), log_gamma)
    r_out = head_group_norm(r_out).reshape(b, L, RET_V_DIM).astype(x.dtype) * jax.nn.silu(g_r)
    merged = jax.nn.sigmoid(gate_a) * (a_out @ w_a_up) + jax.nn.sigmoid(gate_b) * (r_out @ w_r_up)
    h = x + rms_norm(merged @ w_out, norm_post_mix)
    gu = rms_norm(h, norm_pre_ffn) @ w_ffn_in
    g, up = jnp.split(gu, 2, axis=-1)
    h = h + rms_norm((jax.nn.silu(g) * up) @ w_ffn_out, norm_post_ffn)
    h = h + jax.nn.sigmoid(rms_norm(h, norm_ple) @ w_ple_gate) * (ple @ w_ple)
    return h, new_k, new_v, new_state.astype(x.dtype)


def setup_inputs(seed: int = 0) -> dict:
    key = jax.random.key(seed)
    ks = jax.random.split(key, 24)
    f32 = jnp.float32

    def nrm(k, shape, scale):
        return jax.random.normal(k, shape, f32) * scale

    def gain(k):
        return 1.0 + 0.05 * jax.random.normal(k, (DEPTH, D_MODEL), f32)

    state_scale = jnp.sqrt(jnp.minimum(jnp.exp2(5.0 + jnp.arange(N_RET_HEADS, dtype=f32)), float(PAST_LEN)) / RET_DK)
    return {
        'x_prompt': nrm(ks[0], (BATCH, SEQ, D_MODEL), 1.0),
        'x_sample': nrm(ks[1], (DEC_BATCH, DEC_SEQ, D_MODEL), 1.0),
        'cache_swa_k': nrm(ks[2], (DEPTH, DEC_BATCH, SWA_ROWS, N_KV_HEADS, HEAD_DIM), 1.0),
        'cache_swa_v': nrm(ks[3], (DEPTH, DEC_BATCH, SWA_ROWS, N_KV_HEADS, HEAD_DIM), 1.0),
        'state_ret': nrm(ks[4], (DEPTH, DEC_BATCH, N_RET_HEADS, RET_DK, RET_DV), 1.0) * state_scale[None, None, :, None, None],
        'p_prompt': nrm(ks[5], (DEPTH, BATCH, SEQ, PLE_DIM), 1.0),
        'p_sample': nrm(ks[6], (DEPTH, DEC_BATCH, DEC_SEQ, PLE_DIM), 1.0),
        'norm_pre_mix': gain(ks[7]),
        'w_in': nrm(ks[8], (DEPTH, D_MODEL, D_IN), D_MODEL ** -0.5),
        'attn_sinks': nrm(ks[9], (DEPTH, N_Q_HEADS), 0.5),
        'w_a_up': nrm(ks[10], (DEPTH, Q_A_DIM, D_MODEL), Q_A_DIM ** -0.5),
        'w_r_up': nrm(ks[11], (DEPTH, RET_V_DIM, D_MODEL), RET_V_DIM ** -0.5),
        'w_out': nrm(ks[12], (DEPTH, D_MODEL, D_MODEL), D_MODEL ** -0.5),
        'norm_post_mix': gain(ks[13]),
        'norm_pre_ffn': gain(ks[14]),
        'w_ffn_in': nrm(ks[15], (DEPTH, D_MODEL, 2 * D_FF), D_MODEL ** -0.5),
        'w_ffn_out': nrm(ks[16], (DEPTH, D_FF, D_MODEL), D_FF ** -0.5),
        'norm_post_ffn': gain(ks[17]),
        'norm_ple': gain(ks[18]),
        'w_ple_gate': nrm(ks[19], (DEPTH, D_MODEL, D_MODEL), D_MODEL ** -0.5),
        'w_ple': nrm(ks[20], (DEPTH, PLE_DIM, D_MODEL), PLE_DIM ** -0.5),
    }


def reference(x_prompt, x_sample, cache_swa_k, cache_swa_v, state_ret, p_prompt, p_sample,
              norm_pre_mix, w_in, attn_sinks, w_a_up, w_r_up, w_out, norm_post_mix,
              norm_pre_ffn, w_ffn_in, w_ffn_out, norm_post_ffn, norm_ple, w_ple_gate, w_ple):
    hp, hs = x_prompt, x_sample
    kp_list, vp_list, sp_list, ks_list, vs_list, ss_list = [], [], [], [], [], []
    for l in range(DEPTH):
        lw = (norm_pre_mix[l], w_in[l], attn_sinks[l], w_a_up[l], w_r_up[l], w_out[l],
              norm_post_mix[l], norm_pre_ffn[l], w_ffn_in[l], w_ffn_out[l], norm_post_ffn[l],
              norm_ple[l], w_ple_gate[l], w_ple[l])
        hp, kp, vp, sp = layer_forward(hp, p_prompt[l], *lw)
        hs, k_s, v_s, s_s = layer_forward(hs, p_sample[l], *lw,
                                          cache_k=cache_swa_k[l], cache_v=cache_swa_v[l], ret_state=state_ret[l])
        kp_list.append(kp)
        vp_list.append(vp)
        sp_list.append(sp)
        ks_list.append(k_s)
        vs_list.append(v_s)
        ss_list.append(s_s)
    return (hp, hs, jnp.stack(kp_list), jnp.stack(vp_list), jnp.stack(sp_list),
            jnp.stack(ks_list), jnp.stack(vs_list), jnp.stack(ss_list))
```

```python
import functools

import jax
import jax.numpy as jnp
import numpy as np
from jax import lax
from jax.experimental import pallas as pl
from jax.experimental.pallas import tpu as pltpu

F32 = jnp.float32
BF16 = jnp.bfloat16

D_MODEL = 2048
CHUNK = 64
PLE_DIM = 256
N_Q_HEADS = 16
N_KV_HEADS = 4
GROUP = N_Q_HEADS // N_KV_HEADS
HEAD_DIM = 64
SWA_ROWS = 128
Q_A_DIM = N_Q_HEADS * HEAD_DIM
KV_A_DIM = N_KV_HEADS * HEAD_DIM
N_RET_HEADS = 8
RET_DK = 128
RET_DV = 256
RET_QK_DIM = N_RET_HEADS * RET_DK
RET_V_DIM = N_RET_HEADS * RET_DV
D_FF = ((8 * D_MODEL // 3 + 255) // 256) * 256
D_IN = Q_A_DIM + 2 * KV_A_DIM + 2 * RET_QK_DIM + 2 * RET_V_DIM + 2 * D_MODEL
NEG_INF = -1e30
EPS = 1e-6

KV_COLS = 2 * KV_A_DIM
COL_VR, COL_GR, COL_GA, COL_GB = 0, 1, 2, 3
COL_QA, COL_QR, COL_KR = 8, 9, 10
COL_KV = (D_IN - KV_COLS) // KV_COLS

V7X_VMEM_LIMIT_BYTES = 56 * 1024 * 1024

_ALIBI_SLOPES = [float(v) for v in np.exp2(np.float32(-8.0) * (np.arange(N_Q_HEADS, dtype=np.float32) + 1.0) / N_Q_HEADS)]
_LOG_GAMMA = [float(v) for v in np.log(np.float32(1.0) - np.exp2(np.float32(-5.0) - np.arange(N_RET_HEADS, dtype=np.float32)))]
_ATTN_SCALE = HEAD_DIM ** -0.5
_RET_SCALE = RET_DK ** -0.5


def _dot(a, b):
    return jnp.dot(a, b, preferred_element_type=F32)


def _dot_nt(a, b):
    return lax.dot_general(a, b, (((1,), (1,)), ((), ())), preferred_element_type=F32)


def _dot_tn(a, b):
    return lax.dot_general(a, b, (((0,), (0,)), ((), ())), preferred_element_type=F32)


def _rms(x, g):
    return x * lax.rsqrt(jnp.mean(x * x, axis=-1, keepdims=True) + EPS) * g


def _sigmoid(x):
    return 1.0 / (1.0 + jnp.exp(-x))


def _params():
    return pltpu.CompilerParams(dimension_semantics=("arbitrary", "arbitrary"),
                                vmem_limit_bytes=V7X_VMEM_LIMIT_BYTES)


def _in_proj_kernel(x_ref, g_ref, w_ref, o_ref, kv_ref, u_ref, *, kv_tile, kv_off):
    j = pl.program_id(1)

    @pl.when(j == 0)
    def _():
        u_ref[...] = _rms(x_ref[...], g_ref[...]).astype(BF16)

    acc = _dot(u_ref[...], w_ref[...])
    o_ref[...] = acc.astype(BF16)

    @pl.when(j == kv_tile)
    def _():
        kv_ref[...] = acc[:, kv_off:kv_off + KV_COLS]


def _in_proj(x, g, w, *, tm, tn):
    m = x.shape[0]
    n_tiles = D_IN // tn
    kv_start = D_IN - KV_COLS
    kv_tile, kv_off = kv_start // tn, kv_start % tn
    assert m % tm == 0 and D_IN % tn == 0 and kv_off + KV_COLS <= tn
    return pl.pallas_call(
        functools.partial(_in_proj_kernel, kv_tile=kv_tile, kv_off=kv_off),
        grid=(m // tm, n_tiles),
        in_specs=[
            pl.BlockSpec((tm, D_MODEL), lambda i, j: (i, 0)),
            pl.BlockSpec((1, D_MODEL), lambda i, j: (0, 0)),
            pl.BlockSpec((D_MODEL, tn), lambda i, j: (0, j)),
        ],
        out_specs=[
            pl.BlockSpec((tm, tn), lambda i, j: (i, j)),
            pl.BlockSpec((tm, KV_COLS), lambda i, j: (i, 0)),
        ],
        out_shape=[
            jax.ShapeDtypeStruct((m, D_IN), BF16),
            jax.ShapeDtypeStruct((m, KV_COLS), F32),
        ],
        scratch_shapes=[pltpu.VMEM((tm, D_MODEL), BF16)],
        compiler_params=_params(),
        name="in_proj",
    )(x, g, w)


def _seq_mix_kernel(*refs, t_tok, q_blk, n_t, has_cache):
    if has_cache:
        (vr_ref, gr_ref, qa_ref, qr_ref, kr_ref, kv_ref, ck_ref, cv_ref, st_in_ref, sink_ref,
         a_ref, r_ref, st_out_ref,
         state_ref, kf_ref, dmat_ref, cdec_ref, kdec_ref) = refs
    else:
        (vr_ref, gr_ref, qa_ref, qr_ref, kr_ref, kv_ref, kvh_ref, sink_ref,
         a_ref, r_ref, st_out_ref,
         state_ref, kf_ref, dmat_ref, cdec_ref, kdec_ref) = refs
    b = pl.program_id(0)
    t = pl.program_id(1)
    s_keys = SWA_ROWS + q_blk

    @pl.when((b == 0) & (t == 0))
    def _():
        diff = (lax.broadcasted_iota(jnp.int32, (t_tok, t_tok), 0)
                - lax.broadcasted_iota(jnp.int32, (t_tok, t_tok), 1)).astype(F32)
        row_k = lax.broadcasted_iota(jnp.int32, (t_tok, RET_DK), 0).astype(F32)
        for h in range(N_RET_HEADS):
            lg = _LOG_GAMMA[h]
            dmat_ref[h] = jnp.where(diff >= 0.0, jnp.exp(lg * jnp.maximum(diff, 0.0)), 0.0) * _RET_SCALE
            cdec_ref[h] = jnp.exp(lg * (row_k + 1.0))
            kdec_ref[h] = jnp.exp(lg * (t_tok - 1.0 - row_k)) * _RET_SCALE

    @pl.when(t == 0)
    def _():
        if has_cache:
            state_ref[...] = st_in_ref[...]
        else:
            state_ref[...] = jnp.zeros_like(state_ref)

    if has_cache:
        kf_ref[0:SWA_ROWS, 0:KV_A_DIM] = ck_ref[...].astype(BF16)
        kf_ref[0:SWA_ROWS, KV_A_DIM:KV_COLS] = cv_ref[...].astype(BF16)
    else:
        kf_ref[0:SWA_ROWS, :] = kvh_ref[...]
    kf_ref[SWA_ROWS:SWA_ROWS + t_tok, :] = kv_ref[...]

    qi = lax.broadcasted_iota(jnp.int32, (q_blk, s_keys), 0) + SWA_ROWS
    ki = lax.broadcasted_iota(jnp.int32, (q_blk, s_keys), 1)
    dist = jnp.abs(qi - ki).astype(F32)
    qc = qi // CHUNK
    kc = ki // CHUNK
    band = (kc >= qc - SWA_ROWS // CHUNK) & (kc <= qc)
    for qb in range(t_tok // q_blk):
        r0 = qb * q_blk
        valid = band
        if not has_cache and qb == 0:
            valid = band & (ki + jnp.where(t > 0, SWA_ROWS, 0) >= SWA_ROWS)
        for g in range(N_KV_HEADS):
            k_g = kf_ref[r0:r0 + s_keys, g * HEAD_DIM:(g + 1) * HEAD_DIM]
            v_g = kf_ref[r0:r0 + s_keys, KV_A_DIM + g * HEAD_DIM:KV_A_DIM + (g + 1) * HEAD_DIM]
            for jh in range(GROUP):
                hd = g * GROUP + jh
                q_h = qa_ref[r0:r0 + q_blk, hd * HEAD_DIM:(hd + 1) * HEAD_DIM]
                s = _dot_nt(q_h, k_g) * _ATTN_SCALE - _ALIBI_SLOPES[hd] * dist
                s = jnp.where(valid, s, NEG_INF)
                sink = sink_ref[hd]
                m = jnp.maximum(jnp.max(s, axis=-1, keepdims=True), sink)
                e = jnp.exp(s - m)
                den = jnp.sum(e, axis=-1, keepdims=True) + jnp.exp(sink - m)
                o = _dot(e.astype(BF16), v_g) * (1.0 / den)
                a_ref[r0:r0 + q_blk, hd * HEAD_DIM:(hd + 1) * HEAD_DIM] = o.astype(BF16)

    for h in range(N_RET_HEADS):
        lg = _LOG_GAMMA[h]
        q = qr_ref[:, h * RET_DK:(h + 1) * RET_DK]
        k = kr_ref[:, h * RET_DK:(h + 1) * RET_DK]
        v = vr_ref[:, h * RET_DV:(h + 1) * RET_DV]
        st = state_ref[h]
        sc = _dot_nt(q, k) * dmat_ref[h]
        cdec = cdec_ref[h]
        o = _dot(sc.astype(BF16), v) + _dot(q, st.astype(BF16)) * jnp.concatenate([cdec] * (RET_DV // RET_DK), axis=1)
        kd = (k.astype(F32) * kdec_ref[h]).astype(BF16)
        state_ref[h] = st * float(np.exp(np.float32(lg) * np.float32(t_tok))) + _dot_tn(kd, v)
        c = o - jnp.mean(o, axis=-1, keepdims=True)
        nrm = c * lax.rsqrt(jnp.mean(c * c, axis=-1, keepdims=True) + EPS)
        gr = gr_ref[:, h * RET_DV:(h + 1) * RET_DV].astype(F32)
        r_ref[:, h * RET_DV:(h + 1) * RET_DV] = (nrm * (gr * _sigmoid(gr))).astype(BF16)

    @pl.when(t == n_t - 1)
    def _():
        st_out_ref[...] = state_ref[...]


def _seq_mix(proj, sinks, *, n_b, n_t, t_tok, q_blk, cache=None):
    has_cache = cache is not None
    assert t_tok % q_blk == 0 and (has_cache or q_blk == SWA_ROWS)
    row = lambda b, t: b * n_t + t
    halo_per_tile = t_tok // SWA_ROWS if not has_cache else 0
    in_specs = [
        pl.BlockSpec((t_tok, RET_V_DIM), lambda b, t: (row(b, t), COL_VR)),
        pl.BlockSpec((t_tok, RET_V_DIM), lambda b, t: (row(b, t), COL_GR)),
        pl.BlockSpec((t_tok, Q_A_DIM), lambda b, t: (row(b, t), COL_QA)),
        pl.BlockSpec((t_tok, RET_QK_DIM), lambda b, t: (row(b, t), COL_QR)),
        pl.BlockSpec((t_tok, RET_QK_DIM), lambda b, t: (row(b, t), COL_KR)),
        pl.BlockSpec((t_tok, KV_COLS), lambda b, t: (row(b, t), COL_KV)),
    ]
    args = [proj] * 6
    if has_cache:
        ck, cv, st_in = cache
        in_specs += [
            pl.BlockSpec((None, SWA_ROWS, KV_A_DIM), lambda b, t: (b, 0, 0)),
            pl.BlockSpec((None, SWA_ROWS, KV_A_DIM), lambda b, t: (b, 0, 0)),
            pl.BlockSpec((None, N_RET_HEADS, RET_DK, RET_DV), lambda b, t: (b, 0, 0, 0)),
        ]
        args += [ck, cv, st_in]
    else:
        in_specs.append(pl.BlockSpec(
            (SWA_ROWS, KV_COLS), lambda b, t: (jnp.maximum(row(b, t) * halo_per_tile - 1, 0), COL_KV)))
        args.append(proj)
    in_specs.append(pl.BlockSpec(memory_space=pltpu.SMEM))
    args.append(sinks)
    m = n_b * n_t * t_tok
    return pl.pallas_call(
        functools.partial(_seq_mix_kernel, t_tok=t_tok, q_blk=q_blk, n_t=n_t, has_cache=has_cache),
        grid=(n_b, n_t),
        in_specs=in_specs,
        out_specs=[
            pl.BlockSpec((t_tok, Q_A_DIM), lambda b, t: (row(b, t), 0)),
            pl.BlockSpec((t_tok, RET_V_DIM), lambda b, t: (row(b, t), 0)),
            pl.BlockSpec((None, N_RET_HEADS, RET_DK, RET_DV), lambda b, t: (b, 0, 0, 0)),
        ],
        out_shape=[
            jax.ShapeDtypeStruct((m, Q_A_DIM), BF16),
            jax.ShapeDtypeStruct((m, RET_V_DIM), BF16),
            jax.ShapeDtypeStruct((n_b, N_RET_HEADS, RET_DK, RET_DV), F32),
        ],
        scratch_shapes=[
            pltpu.VMEM((N_RET_HEADS, RET_DK, RET_DV), F32),
            pltpu.VMEM((SWA_ROWS + t_tok, KV_COLS), BF16),
            pltpu.VMEM((N_RET_HEADS, t_tok, t_tok), F32),
            pltpu.VMEM((N_RET_HEADS, t_tok, RET_DK), F32),
            pltpu.VMEM((N_RET_HEADS, t_tok, RET_DK), F32),
        ],
        compiler_params=_params(),
        name="seq_mix_sample" if has_cache else "seq_mix_prompt",
    )(*args)


def _merge_kernel(x_ref, a_ref, r_ref, ga_ref, gb_ref, wau_ref, wru_ref, wout_ref, gpost_ref, h_ref, acc_ref, *, n_n):
    j = pl.program_id(1)

    @pl.when(j == 0)
    def _():
        acc_ref[...] = jnp.zeros_like(acc_ref)

    merged = (_sigmoid(ga_ref[...].astype(F32)) * _dot(a_ref[...], wau_ref[...])
              + _sigmoid(gb_ref[...].astype(F32)) * _dot(r_ref[...], wru_ref[...]))
    acc_ref[...] += _dot(merged.astype(BF16), wout_ref[...])

    @pl.when(j == n_n - 1)
    def _():
        h_ref[...] = x_ref[...] + _rms(acc_ref[...], gpost_ref[...])


def _merge(x, a, r, proj, wau, wru, wout, gpost, *, tm, tn):
    m = x.shape[0]
    n_n = D_MODEL // tn
    assert m % tm == 0 and D_MODEL % tn == 0
    return pl.pallas_call(
        functools.partial(_merge_kernel, n_n=n_n),
        grid=(m // tm, n_n),
        in_specs=[
            pl.BlockSpec((tm, D_MODEL), lambda i, j: (i, 0)),
            pl.BlockSpec((tm, Q_A_DIM), lambda i, j: (i, 0)),
            pl.BlockSpec((tm, RET_V_DIM), lambda i, j: (i, 0)),
            pl.BlockSpec((tm, tn), lambda i, j: (i, COL_GA * n_n + j)),
            pl.BlockSpec((tm, tn), lambda i, j: (i, COL_GB * n_n + j)),
            pl.BlockSpec((Q_A_DIM, tn), lambda i, j: (0, j)),
            pl.BlockSpec((RET_V_DIM, tn), lambda i, j: (0, j)),
            pl.BlockSpec((tn, D_MODEL), lambda i, j: (j, 0)),
            pl.BlockSpec((1, D_MODEL), lambda i, j: (0, 0)),
        ],
        out_specs=pl.BlockSpec((tm, D_MODEL), lambda i, j: (i, 0)),
        out_shape=jax.ShapeDtypeStruct((m, D_MODEL), F32),
        scratch_shapes=[pltpu.VMEM((tm, D_MODEL), F32)],
        compiler_params=_params(),
        name="merge",
    )(x, a, r, proj, proj, wau, wru, wout, gpost)


def _ffn_kernel(h_ref, p_ref, wg_ref, wu_ref, wo_ref, gpre_ref, gpost_ref, gple_ref, wpg_ref, wple_ref,
                y_ref, t_ref, acc_ref, *, n_f):
    j = pl.program_id(1)

    @pl.when(j == 0)
    def _():
        t_ref[...] = _rms(h_ref[...], gpre_ref[...]).astype(BF16)
        acc_ref[...] = jnp.zeros_like(acc_ref)

    tt = t_ref[...]
    g = _dot(tt, wg_ref[...])
    up = _dot(tt, wu_ref[...])
    act = (g * _sigmoid(g) * up).astype(BF16)
    acc_ref[...] += _dot(act, wo_ref[...])

    @pl.when(j == n_f - 1)
    def _():
        h2 = h_ref[...] + _rms(acc_ref[...], gpost_ref[...])
        z = _rms(h2, gple_ref[...]).astype(BF16)
        gate = _sigmoid(_dot(z, wpg_ref[...]))
        y_ref[...] = h2 + gate * _dot(p_ref[...].astype(BF16), wple_ref[...])


def _ffn(h, ple, wff_in, wff_out, gpre, gpost, gple, wpg, wple, *, tm, tf):
    m = h.shape[0]
    n_f = D_FF // tf
    assert m % tm == 0 and D_FF % tf == 0
    const = lambda shape: pl.BlockSpec(shape, lambda i, j: (0,) * len(shape), pipeline_mode=pl.Buffered(1))
    return pl.pallas_call(
        functools.partial(_ffn_kernel, n_f=n_f),
        grid=(m // tm, n_f),
        in_specs=[
            pl.BlockSpec((tm, D_MODEL), lambda i, j: (i, 0)),
            pl.BlockSpec((tm, PLE_DIM), lambda i, j: (i, 0)),
            pl.BlockSpec((D_MODEL, tf), lambda i, j: (0, j)),
            pl.BlockSpec((D_MODEL, tf), lambda i, j: (0, j + n_f)),
            pl.BlockSpec((tf, D_MODEL), lambda i, j: (j, 0)),
            const((1, D_MODEL)),
            const((1, D_MODEL)),
            const((1, D_MODEL)),
            const((D_MODEL, D_MODEL)),
            const((PLE_DIM, D_MODEL)),
        ],
        out_specs=pl.BlockSpec((tm, D_MODEL), lambda i, j: (i, 0)),
        out_shape=jax.ShapeDtypeStruct((m, D_MODEL), F32),
        scratch_shapes=[pltpu.VMEM((tm, D_MODEL), BF16), pltpu.VMEM((tm, D_MODEL), F32)],
        compiler_params=_params(),
        name="ffn_ple",
    )(h, ple, wff_in, wff_in, wff_out, gpre, gpost, gple, wpg, wple)


def _layer(x, ple, w, *, n_b, seq, cache=None):
    proj, kv = _in_proj(x, w["g_pre_mix"], w["w_in"], tm=512, tn=D_IN // 4)
    if cache is None:
        a, r, state = _seq_mix(proj, w["sinks"], n_b=n_b, n_t=seq // 256, t_tok=256, q_blk=SWA_ROWS)
    else:
        a, r, state = _seq_mix(proj, w["sinks"], n_b=n_b, n_t=1, t_tok=seq, q_blk=seq, cache=cache)
    h = _merge(x, a, r, proj, w["w_a_up"], w["w_r_up"], w["w_out"], w["g_post_mix"], tm=512, tn=512)
    y = _ffn(h, ple, w["w_ffn_in"], w["w_ffn_out"], w["g_pre_ffn"], w["g_post_ffn"], w["g_ple"],
             w["w_ple_gate"], w["w_ple"], tm=512, tf=512)
    kv = kv.reshape(n_b, seq, KV_COLS)
    return y, kv[:, :, :KV_A_DIM], kv[:, :, KV_A_DIM:], state


def kernel(x_prompt, x_sample, cache_swa_k, cache_swa_v, state_ret, p_prompt, p_sample, norm_pre_mix, w_in, attn_sinks, w_a_up, w_r_up, w_out, norm_post_mix, norm_pre_ffn, w_ffn_in, w_ffn_out, norm_post_ffn, norm_ple, w_ple_gate, w_ple):
    depth = w_in.shape[0]
    n_bp, seq_p, _ = x_prompt.shape
    n_bs, seq_s, _ = x_sample.shape
    hp = x_prompt.reshape(n_bp * seq_p, D_MODEL)
    hs = x_sample.reshape(n_bs * seq_s, D_MODEL)
    kv_start = Q_A_DIM
    ret_start = Q_A_DIM + KV_COLS
    outs = [[] for _ in range(6)]
    for l in range(depth):
        wl = w_in[l]
        w = {
            "w_in": jnp.concatenate([wl[:, ret_start + 2 * RET_QK_DIM:], wl[:, :Q_A_DIM],
                                     wl[:, ret_start:ret_start + 2 * RET_QK_DIM],
                                     wl[:, kv_start:ret_start]], axis=1).astype(BF16),
            "w_a_up": w_a_up[l].astype(BF16),
            "w_r_up": w_r_up[l].astype(BF16),
            "w_out": w_out[l].astype(BF16),
            "w_ffn_in": w_ffn_in[l].astype(BF16),
            "w_ffn_out": w_ffn_out[l].astype(BF16),
            "w_ple_gate": w_ple_gate[l].astype(BF16),
            "w_ple": w_ple[l].astype(BF16),
            "sinks": attn_sinks[l],
            "g_pre_mix": norm_pre_mix[l].reshape(1, D_MODEL),
            "g_post_mix": norm_post_mix[l].reshape(1, D_MODEL),
            "g_pre_ffn": norm_pre_ffn[l].reshape(1, D_MODEL),
            "g_post_ffn": norm_post_ffn[l].reshape(1, D_MODEL),
            "g_ple": norm_ple[l].reshape(1, D_MODEL),
        }
        hp, kp, vp, sp = _layer(hp, p_prompt[l].reshape(n_bp * seq_p, PLE_DIM), w, n_b=n_bp, seq=seq_p)
        cache = (cache_swa_k[l].reshape(n_bs, SWA_ROWS, KV_A_DIM), cache_swa_v[l].reshape(n_bs, SWA_ROWS, KV_A_DIM),
                 state_ret[l])
        hs, k_s, v_s, s_s = _layer(hs, p_sample[l].reshape(n_bs * seq_s, PLE_DIM), w, n_b=n_bs, seq=seq_s, cache=cache)
        kv_shape = (SWA_ROWS, N_KV_HEADS, HEAD_DIM)
        outs[0].append(kp[:, seq_p - SWA_ROWS:].reshape(n_bp, *kv_shape))
        outs[1].append(vp[:, seq_p - SWA_ROWS:].reshape(n_bp, *kv_shape))
        outs[2].append(sp)
        outs[3].append(jnp.concatenate([cache[0][:, seq_s:], k_s], axis=1).reshape(n_bs, *kv_shape))
        outs[4].append(jnp.concatenate([cache[1][:, seq_s:], v_s], axis=1).reshape(n_bs, *kv_shape))
        outs[5].append(s_s)
    return (hp.reshape(n_bp, seq_p, D_MODEL), hs.reshape(n_bs, seq_s, D_MODEL),
            *(jnp.stack(o) for o in outs))
```

```python
import functools

import jax
import jax.numpy as jnp
import numpy as np
from jax import lax
from jax.experimental import pallas as pl
from jax.experimental.pallas import tpu as pltpu

F32 = jnp.float32
BF16 = jnp.bfloat16

D_MODEL = 2048
CHUNK = 64
PLE_DIM = 256
N_Q_HEADS = 16
N_KV_HEADS = 4
GROUP = N_Q_HEADS // N_KV_HEADS
HEAD_DIM = 64
SWA_ROWS = 128
Q_A_DIM = N_Q_HEADS * HEAD_DIM
KV_A_DIM = N_KV_HEADS * HEAD_DIM
N_RET_HEADS = 8
RET_DK = 128
RET_DV = 256
RET_QK_DIM = N_RET_HEADS * RET_DK
RET_V_DIM = N_RET_HEADS * RET_DV
D_FF = ((8 * D_MODEL // 3 + 255) // 256) * 256
D_IN = Q_A_DIM + 2 * KV_A_DIM + 2 * RET_QK_DIM + 2 * RET_V_DIM + 2 * D_MODEL
NEG_INF = -1e30
EPS = 1e-6

KV_COLS = 2 * KV_A_DIM
COL_VR, COL_GR, COL_GA, COL_GB = 0, 1, 2, 3
COL_QA, COL_QR, COL_KR = 8, 9, 10
COL_KV = (D_IN - KV_COLS) // KV_COLS

V7X_VMEM_LIMIT_BYTES = 60 * 1024 * 1024

_ALIBI_SLOPES = [float(v) for v in np.exp2(np.float32(-8.0) * (np.arange(N_Q_HEADS, dtype=np.float32) + 1.0) / N_Q_HEADS)]
_LOG_GAMMA = [float(v) for v in np.log(np.float32(1.0) - np.exp2(np.float32(-5.0) - np.arange(N_RET_HEADS, dtype=np.float32)))]
_ATTN_SCALE = HEAD_DIM ** -0.5
_RET_SCALE = RET_DK ** -0.5


def _dot(a, b):
    return jnp.dot(a, b, preferred_element_type=F32)


def _dot_nt(a, b):
    return lax.dot_general(a, b, (((1,), (1,)), ((), ())), preferred_element_type=F32)


def _dot_tn(a, b):
    return lax.dot_general(a, b, (((0,), (0,)), ((), ())), preferred_element_type=F32)


def _rms(x, g):
    return x * lax.rsqrt(jnp.mean(x * x, axis=-1, keepdims=True) + EPS) * g


def _sigmoid(x):
    return 1.0 / (1.0 + jnp.exp(-x))


def _params(grid_rank=2):
    return pltpu.CompilerParams(dimension_semantics=("arbitrary",) * grid_rank,
                                vmem_limit_bytes=V7X_VMEM_LIMIT_BYTES)


def _in_proj_kernel(x_ref, g_ref, w_ref, o_ref, kv_ref, u_ref, *, kv_tile, kv_off):
    j = pl.program_id(1)

    @pl.when(j == 0)
    def _():
        u_ref[...] = _rms(x_ref[...], g_ref[...]).astype(BF16)

    acc = _dot(u_ref[...], w_ref[...])
    o_ref[...] = acc.astype(BF16)

    @pl.when(j == kv_tile)
    def _():
        kv_ref[...] = acc[:, kv_off:kv_off + KV_COLS]


def _in_proj(x, g, w, *, tm, tn):
    m = x.shape[0]
    n_tiles = D_IN // tn
    kv_start = D_IN - KV_COLS
    kv_tile, kv_off = kv_start // tn, kv_start % tn
    assert m % tm == 0 and D_IN % tn == 0 and kv_off + KV_COLS <= tn
    return pl.pallas_call(
        functools.partial(_in_proj_kernel, kv_tile=kv_tile, kv_off=kv_off),
        grid=(m // tm, n_tiles),
        in_specs=[
            pl.BlockSpec((tm, D_MODEL), lambda i, j: (i, 0)),
            pl.BlockSpec((1, D_MODEL), lambda i, j: (0, 0)),
            pl.BlockSpec((D_MODEL, tn), lambda i, j: (0, j)),
        ],
        out_specs=[
            pl.BlockSpec((tm, tn), lambda i, j: (i, j)),
            pl.BlockSpec((tm, KV_COLS), lambda i, j: (i, 0)),
        ],
        out_shape=[
            jax.ShapeDtypeStruct((m, D_IN), BF16),
            jax.ShapeDtypeStruct((m, KV_COLS), F32),
        ],
        scratch_shapes=[pltpu.VMEM((tm, D_MODEL), BF16)],
        compiler_params=_params(),
        name="in_proj",
    )(x, g, w)


def _seq_mix_kernel(*refs, t_tok, q_blk, n_t, has_cache):
    if has_cache:
        (vr_ref, gr_ref, qa_ref, qr_ref, kr_ref, kv_ref, ck_ref, cv_ref, st_in_ref, sink_ref,
         a_ref, r_ref, st_out_ref,
         state_ref, kf_ref, dmat_ref, cdec_ref, kdec_ref, bias_ref) = refs
    else:
        (vr_ref, gr_ref, qa_ref, qr_ref, kr_ref, kv_ref, kvh_ref, sink_ref,
         a_ref, r_ref, st_out_ref,
         state_ref, kf_ref, dmat_ref, cdec_ref, kdec_ref, bias_ref) = refs
    b = pl.program_id(0)
    t = pl.program_id(1)
    s_keys = SWA_ROWS + q_blk

    @pl.when((b == 0) & (t == 0))
    def _():
        qi = lax.broadcasted_iota(jnp.int32, (q_blk, s_keys), 0) + SWA_ROWS
        ki = lax.broadcasted_iota(jnp.int32, (q_blk, s_keys), 1)
        dist = jnp.abs(qi - ki).astype(F32)
        qc = qi // CHUNK
        kc = ki // CHUNK
        band = (kc >= qc - SWA_ROWS // CHUNK) & (kc <= qc)
        for hd in range(N_Q_HEADS):
            alibi = -_ALIBI_SLOPES[hd] * dist
            bias_ref[hd] = jnp.where(band, alibi, NEG_INF)
            if not has_cache:
                bias_ref[N_Q_HEADS + hd] = jnp.where(band & (ki >= SWA_ROWS), alibi, NEG_INF)

        diff = (lax.broadcasted_iota(jnp.int32, (t_tok, t_tok), 0)
                - lax.broadcasted_iota(jnp.int32, (t_tok, t_tok), 1)).astype(F32)
        row_k = lax.broadcasted_iota(jnp.int32, (t_tok, RET_DK), 0).astype(F32)
        for h in range(N_RET_HEADS):
            lg = _LOG_GAMMA[h]
            dmat_ref[h] = jnp.where(diff >= 0.0, jnp.exp(lg * jnp.maximum(diff, 0.0)), 0.0) * _RET_SCALE
            cdec_ref[h] = jnp.exp(lg * (row_k + 1.0))
            kdec_ref[h] = jnp.exp(lg * (t_tok - 1.0 - row_k)) * _RET_SCALE

    @pl.when(t == 0)
    def _():
        if has_cache:
            state_ref[...] = st_in_ref[...]
        else:
            state_ref[...] = jnp.zeros_like(state_ref)

    if has_cache:
        kf_ref[0:SWA_ROWS, 0:KV_A_DIM] = ck_ref[...].astype(BF16)
        kf_ref[0:SWA_ROWS, KV_A_DIM:KV_COLS] = cv_ref[...].astype(BF16)
    else:
        kf_ref[0:SWA_ROWS, :] = kvh_ref[...]
    kf_ref[SWA_ROWS:SWA_ROWS + t_tok, :] = kv_ref[...]

    def attention_group(qb, g):
        r0 = qb * q_blk
        first_blk = N_Q_HEADS * jnp.where(t == 0, 1, 0) if (not has_cache and qb == 0) else 0
        k_g = kf_ref[r0:r0 + s_keys, g * HEAD_DIM:(g + 1) * HEAD_DIM]
        v_g = kf_ref[r0:r0 + s_keys, KV_A_DIM + g * HEAD_DIM:KV_A_DIM + (g + 1) * HEAD_DIM]
        q_g = qa_ref[r0:r0 + q_blk, g * GROUP * HEAD_DIM:(g + 1) * GROUP * HEAD_DIM] * jnp.asarray(_ATTN_SCALE, BF16)
        for jh in range(GROUP):
            hd = g * GROUP + jh
            q_h = q_g[:, jh * HEAD_DIM:(jh + 1) * HEAD_DIM]
            s = _dot_nt(q_h, k_g) + bias_ref[hd + first_blk]
            sink = sink_ref[hd]
            m = jnp.maximum(jnp.max(s, axis=-1, keepdims=True), sink)
            e = jnp.exp(s - m)
            den = jnp.sum(e, axis=-1, keepdims=True) + jnp.exp(sink - m)
            o = _dot(e.astype(BF16), v_g) * (1.0 / den)
            a_ref[r0:r0 + q_blk, hd * HEAD_DIM:(hd + 1) * HEAD_DIM] = o.astype(BF16)

    def retention_head(h):
        lg = _LOG_GAMMA[h]
        q = qr_ref[:, h * RET_DK:(h + 1) * RET_DK]
        k = kr_ref[:, h * RET_DK:(h + 1) * RET_DK]
        v = vr_ref[:, h * RET_DV:(h + 1) * RET_DV]
        st = state_ref[h]
        sc = _dot_nt(q, k) * dmat_ref[h]
        cdec = cdec_ref[h]
        o = _dot(sc.astype(BF16), v) + _dot(q, st.astype(BF16)) * jnp.concatenate([cdec] * (RET_DV // RET_DK), axis=1)
        kd = (k.astype(F32) * kdec_ref[h]).astype(BF16)
        state_ref[h] = st * float(np.exp(np.float32(lg) * np.float32(t_tok))) + _dot_tn(kd, v)
        c = o - jnp.mean(o, axis=-1, keepdims=True)
        nrm = c * lax.rsqrt(jnp.mean(c * c, axis=-1, keepdims=True) + EPS)
        gr = gr_ref[:, h * RET_DV:(h + 1) * RET_DV].astype(F32)
        r_ref[:, h * RET_DV:(h + 1) * RET_DV] = (nrm * (gr * _sigmoid(gr))).astype(BF16)

    attn_groups = [(qb, g) for qb in range(t_tok // q_blk) for g in range(N_KV_HEADS)]
    for i in range(max(len(attn_groups), N_RET_HEADS)):
        if i < N_RET_HEADS:
            retention_head(i)
        if i < len(attn_groups):
            attention_group(*attn_groups[i])

    @pl.when(t == n_t - 1)
    def _():
        st_out_ref[...] = state_ref[...]


def _seq_mix(proj, sinks, *, n_b, n_t, t_tok, q_blk, cache=None):
    has_cache = cache is not None
    assert t_tok % q_blk == 0 and (has_cache or q_blk == SWA_ROWS)
    row = lambda b, t: b * n_t + t
    halo_per_tile = t_tok // SWA_ROWS if not has_cache else 0
    in_specs = [
        pl.BlockSpec((t_tok, RET_V_DIM), lambda b, t: (row(b, t), COL_VR)),
        pl.BlockSpec((t_tok, RET_V_DIM), lambda b, t: (row(b, t), COL_GR)),
        pl.BlockSpec((t_tok, Q_A_DIM), lambda b, t: (row(b, t), COL_QA)),
        pl.BlockSpec((t_tok, RET_QK_DIM), lambda b, t: (row(b, t), COL_QR)),
        pl.BlockSpec((t_tok, RET_QK_DIM), lambda b, t: (row(b, t), COL_KR)),
        pl.BlockSpec((t_tok, KV_COLS), lambda b, t: (row(b, t), COL_KV)),
    ]
    args = [proj] * 6
    if has_cache:
        ck, cv, st_in = cache
        in_specs += [
            pl.BlockSpec((None, SWA_ROWS, KV_A_DIM), lambda b, t: (b, 0, 0)),
            pl.BlockSpec((None, SWA_ROWS, KV_A_DIM), lambda b, t: (b, 0, 0)),
            pl.BlockSpec((None, N_RET_HEADS, RET_DK, RET_DV), lambda b, t: (b, 0, 0, 0)),
        ]
        args += [ck, cv, st_in]
    else:
        in_specs.append(pl.BlockSpec(
            (SWA_ROWS, KV_COLS), lambda b, t: (jnp.maximum(row(b, t) * halo_per_tile - 1, 0), COL_KV)))
        args.append(proj)
    in_specs.append(pl.BlockSpec(memory_space=pltpu.SMEM))
    args.append(sinks)
    m = n_b * n_t * t_tok
    return pl.pallas_call(
        functools.partial(_seq_mix_kernel, t_tok=t_tok, q_blk=q_blk, n_t=n_t, has_cache=has_cache),
        grid=(n_b, n_t),
        in_specs=in_specs,
        out_specs=[
            pl.BlockSpec((t_tok, Q_A_DIM), lambda b, t: (row(b, t), 0)),
            pl.BlockSpec((t_tok, RET_V_DIM), lambda b, t: (row(b, t), 0)),
            pl.BlockSpec((None, N_RET_HEADS, RET_DK, RET_DV), lambda b, t: (b, 0, 0, 0)),
        ],
        out_shape=[
            jax.ShapeDtypeStruct((m, Q_A_DIM), BF16),
            jax.ShapeDtypeStruct((m, RET_V_DIM), BF16),
            jax.ShapeDtypeStruct((n_b, N_RET_HEADS, RET_DK, RET_DV), F32),
        ],
        scratch_shapes=[
            pltpu.VMEM((N_RET_HEADS, RET_DK, RET_DV), F32),
            pltpu.VMEM((SWA_ROWS + t_tok, KV_COLS), BF16),
            pltpu.VMEM((N_RET_HEADS, t_tok, t_tok), F32),
            pltpu.VMEM((N_RET_HEADS, t_tok, RET_DK), F32),
            pltpu.VMEM((N_RET_HEADS, t_tok, RET_DK), F32),
            pltpu.VMEM(((1 if has_cache else 2) * N_Q_HEADS, q_blk, SWA_ROWS + q_blk), F32),
        ],
        compiler_params=_params(),
        name="seq_mix_sample" if has_cache else "seq_mix_prompt",
    )(*args)


def _merge_kernel(x_ref, a_ref, r_ref, ga_ref, gb_ref, wau_ref, wru_ref, wout_ref, gpost_ref, h_ref):
    merged = (_sigmoid(ga_ref[...].astype(F32)) * _dot(a_ref[...], wau_ref[...])
              + _sigmoid(gb_ref[...].astype(F32)) * _dot(r_ref[...], wru_ref[...]))
    mo = _dot(merged.astype(BF16), wout_ref[...])
    h_ref[...] = x_ref[...] + _rms(mo, gpost_ref[...])


def _merge(x, a, r, proj, wau, wru, wout, gpost, *, tm):
    m = x.shape[0]
    assert m % tm == 0
    const = lambda shape: pl.BlockSpec(shape, lambda i: (0,) * len(shape), pipeline_mode=pl.Buffered(1))
    return pl.pallas_call(
        _merge_kernel,
        grid=(m // tm,),
        in_specs=[
            pl.BlockSpec((tm, D_MODEL), lambda i: (i, 0)),
            pl.BlockSpec((tm, Q_A_DIM), lambda i: (i, 0)),
            pl.BlockSpec((tm, RET_V_DIM), lambda i: (i, 0)),
            pl.BlockSpec((tm, D_MODEL), lambda i: (i, COL_GA)),
            pl.BlockSpec((tm, D_MODEL), lambda i: (i, COL_GB)),
            const((Q_A_DIM, D_MODEL)),
            const((RET_V_DIM, D_MODEL)),
            const((D_MODEL, D_MODEL)),
            const((1, D_MODEL)),
        ],
        out_specs=pl.BlockSpec((tm, D_MODEL), lambda i: (i, 0)),
        out_shape=jax.ShapeDtypeStruct((m, D_MODEL), F32),
        compiler_params=_params(1),
        name="merge",
    )(x, a, r, proj, proj, wau, wru, wout, gpost)


def _ffn_kernel(h_ref, p_ref, wg_ref, wu_ref, wo_ref, gpre_ref, gpost_ref, gple_ref, wpg_ref, wple_ref,
                y_ref, t_ref, acc_ref, *, n_f):
    j = pl.program_id(1)

    @pl.when(j == 0)
    def _():
        t_ref[...] = _rms(h_ref[...], gpre_ref[...]).astype(BF16)
        acc_ref[...] = jnp.zeros_like(acc_ref)

    tt = t_ref[...]
    g = _dot(tt, wg_ref[...])
    up = _dot(tt, wu_ref[...])
    act = (g * _sigmoid(g) * up).astype(BF16)
    acc_ref[...] += _dot(act, wo_ref[...])

    @pl.when(j == n_f - 1)
    def _():
        h2 = h_ref[...] + _rms(acc_ref[...], gpost_ref[...])
        z = _rms(h2, gple_ref[...]).astype(BF16)
        gate = _sigmoid(_dot(z, wpg_ref[...]))
        y_ref[...] = h2 + gate * _dot(p_ref[...].astype(BF16), wple_ref[...])


def _ffn(h, ple, wff_in, wff_out, gpre, gpost, gple, wpg, wple, *, tm, tf):
    m = h.shape[0]
    n_f = D_FF // tf
    assert m % tm == 0 and D_FF % tf == 0
    const = lambda shape: pl.BlockSpec(shape, lambda i, j: (0,) * len(shape), pipeline_mode=pl.Buffered(1))
    return pl.pallas_call(
        functools.partial(_ffn_kernel, n_f=n_f),
        grid=(m // tm, n_f),
        in_specs=[
            pl.BlockSpec((tm, D_MODEL), lambda i, j: (i, 0)),
            pl.BlockSpec((tm, PLE_DIM), lambda i, j: (i, 0)),
            pl.BlockSpec((D_MODEL, tf), lambda i, j: (0, j)),
            pl.BlockSpec((D_MODEL, tf), lambda i, j: (0, j + n_f)),
            pl.BlockSpec((tf, D_MODEL), lambda i, j: (j, 0)),
            const((1, D_MODEL)),
            const((1, D_MODEL)),
            const((1, D_MODEL)),
            const((D_MODEL, D_MODEL)),
            const((PLE_DIM, D_MODEL)),
        ],
        out_specs=pl.BlockSpec((tm, D_MODEL), lambda i, j: (i, 0)),
        out_shape=jax.ShapeDtypeStruct((m, D_MODEL), F32),
        scratch_shapes=[pltpu.VMEM((tm, D_MODEL), BF16), pltpu.VMEM((tm, D_MODEL), F32)],
        compiler_params=_params(),
        name="ffn_ple",
    )(h, ple, wff_in, wff_in, wff_out, gpre, gpost, gple, wpg, wple)


def _layer(x, ple, w, *, n_b, seq, cache=None):
    proj, kv = _in_proj(x, w["g_pre_mix"], w["w_in"], tm=512, tn=D_IN // 4)
    if cache is None:
        a, r, state = _seq_mix(proj, w["sinks"], n_b=n_b, n_t=seq // 256, t_tok=256, q_blk=SWA_ROWS)
    else:
        a, r, state = _seq_mix(proj, w["sinks"], n_b=n_b, n_t=1, t_tok=seq, q_blk=seq, cache=cache)
    h = _merge(x, a, r, proj, w["w_a_up"], w["w_r_up"], w["w_out"], w["g_post_mix"], tm=512)
    y = _ffn(h, ple, w["w_ffn_in"], w["w_ffn_out"], w["g_pre_ffn"], w["g_post_ffn"], w["g_ple"],
             w["w_ple_gate"], w["w_ple"], tm=512, tf=512)
    kv = kv.reshape(n_b, seq, KV_COLS)[:, max(seq - SWA_ROWS, 0):]
    return y, kv[:, :, :KV_A_DIM], kv[:, :, KV_A_DIM:], state


def kernel(x_prompt, x_sample, cache_swa_k, cache_swa_v, state_ret, p_prompt, p_sample, norm_pre_mix, w_in, attn_sinks, w_a_up, w_r_up, w_out, norm_post_mix, norm_pre_ffn, w_ffn_in, w_ffn_out, norm_post_ffn, norm_ple, w_ple_gate, w_ple):
    depth = w_in.shape[0]
    n_bp, seq_p, _ = x_prompt.shape
    n_bs, seq_s, _ = x_sample.shape
    hp = x_prompt.reshape(n_bp * seq_p, D_MODEL)
    hs = x_sample.reshape(n_bs * seq_s, D_MODEL)
    kv_start = Q_A_DIM
    ret_start = Q_A_DIM + KV_COLS
    outs = [[] for _ in range(6)]
    for l in range(depth):
        wl = w_in[l]
        w = {
            "w_in": jnp.concatenate([wl[:, ret_start + 2 * RET_QK_DIM:], wl[:, :Q_A_DIM],
                                     wl[:, ret_start:ret_start + 2 * RET_QK_DIM],
                                     wl[:, kv_start:ret_start]], axis=1).astype(BF16),
            "w_a_up": w_a_up[l].astype(BF16),
            "w_r_up": w_r_up[l].astype(BF16),
            "w_out": w_out[l].astype(BF16),
            "w_ffn_in": w_ffn_in[l].astype(BF16),
            "w_ffn_out": w_ffn_out[l].astype(BF16),
            "w_ple_gate": w_ple_gate[l].astype(BF16),
            "w_ple": w_ple[l].astype(BF16),
            "sinks": attn_sinks[l],
            "g_pre_mix": norm_pre_mix[l].reshape(1, D_MODEL),
            "g_post_mix": norm_post_mix[l].reshape(1, D_MODEL),
            "g_pre_ffn": norm_pre_ffn[l].reshape(1, D_MODEL),
            "g_post_ffn": norm_post_ffn[l].reshape(1, D_MODEL),
            "g_ple": norm_ple[l].reshape(1, D_MODEL),
        }
        hp, kp, vp, sp = _layer(hp, p_prompt[l].reshape(n_bp * seq_p, PLE_DIM), w, n_b=n_bp, seq=seq_p)
        cache = (cache_swa_k[l].reshape(n_bs, SWA_ROWS, KV_A_DIM), cache_swa_v[l].reshape(n_bs, SWA_ROWS, KV_A_DIM),
                 state_ret[l])
        hs, k_s, v_s, s_s = _layer(hs, p_sample[l].reshape(n_bs * seq_s, PLE_DIM), w, n_b=n_bs, seq=seq_s, cache=cache)
        kv_shape = (SWA_ROWS, N_KV_HEADS, HEAD_DIM)
        outs[0].append(kp.reshape(n_bp, *kv_shape))
        outs[1].append(vp.reshape(n_bp, *kv_shape))
        outs[2].append(sp)
        outs[3].append(jnp.concatenate([cache[0][:, seq_s:], k_s], axis=1).reshape(n_bs, *kv_shape))
        outs[4].append(jnp.concatenate([cache[1][:, seq_s:], v_s], axis=1).reshape(n_bs, *kv_shape))
        outs[5].append(s_s)
    return (hp.reshape(n_bp, seq_p, D_MODEL), hs.reshape(n_bs, seq_s, D_MODEL),
            *(jnp.stack(o) for o in outs))
```

```python
import functools

import jax
import jax.numpy as jnp
import numpy as np
from jax import lax
from jax.experimental import pallas as pl
from jax.experimental.pallas import tpu as pltpu

F32 = jnp.float32
BF16 = jnp.bfloat16

D_MODEL = 2048
CHUNK = 64
PLE_DIM = 256
N_Q_HEADS = 16
N_KV_HEADS = 4
GROUP = N_Q_HEADS // N_KV_HEADS
HEAD_DIM = 64
SWA_ROWS = 128
Q_A_DIM = N_Q_HEADS * HEAD_DIM
KV_A_DIM = N_KV_HEADS * HEAD_DIM
N_RET_HEADS = 8
RET_DK = 128
RET_DV = 256
RET_QK_DIM = N_RET_HEADS * RET_DK
RET_V_DIM = N_RET_HEADS * RET_DV
D_FF = ((8 * D_MODEL // 3 + 255) // 256) * 256
D_IN = Q_A_DIM + 2 * KV_A_DIM + 2 * RET_QK_DIM + 2 * RET_V_DIM + 2 * D_MODEL
NEG_INF = -1e30
EPS = 1e-6

KV_COLS = 2 * KV_A_DIM
COL_VR, COL_GR, COL_GA, COL_GB = 0, 1, 2, 3
COL_QA, COL_QR, COL_KR = 8, 9, 10
COL_KV = (D_IN - KV_COLS) // KV_COLS

V7X_VMEM_LIMIT_BYTES = 60 * 1024 * 1024

_ALIBI_SLOPES = [float(v) for v in np.exp2(np.float32(-8.0) * (np.arange(N_Q_HEADS, dtype=np.float32) + 1.0) / N_Q_HEADS)]
_LOG_GAMMA = [float(v) for v in np.log(np.float32(1.0) - np.exp2(np.float32(-5.0) - np.arange(N_RET_HEADS, dtype=np.float32)))]
_ATTN_SCALE = HEAD_DIM ** -0.5
_RET_SCALE = RET_DK ** -0.5


def _dot(a, b):
    return jnp.dot(a, b, preferred_element_type=F32)


def _dot_nt(a, b):
    return lax.dot_general(a, b, (((1,), (1,)), ((), ())), preferred_element_type=F32)


def _dot_tn(a, b):
    return lax.dot_general(a, b, (((0,), (0,)), ((), ())), preferred_element_type=F32)


def _rms(x, g):
    return x * lax.rsqrt(jnp.mean(x * x, axis=-1, keepdims=True) + EPS) * g


def _sigmoid(x):
    return 1.0 / (1.0 + jnp.exp(-x))


def _params(grid_rank=2):
    return pltpu.CompilerParams(dimension_semantics=("arbitrary",) * grid_rank,
                                vmem_limit_bytes=V7X_VMEM_LIMIT_BYTES)


def _regroup_kernel(w_ref, o_ref):
    o_ref[...] = w_ref[...].astype(BF16)


def _regroup_w_in(w):
    n_tiles = D_IN // KV_COLS
    ret_v_tile = (Q_A_DIM + KV_COLS + 2 * RET_QK_DIM) // KV_COLS
    n_wide = n_tiles - ret_v_tile
    n_q = Q_A_DIM // KV_COLS

    def src_tile(j):
        return jnp.where(j < n_wide, j + ret_v_tile,
                         jnp.where(j < n_wide + n_q, j - n_wide,
                                   jnp.where(j < n_tiles - 1, j - n_wide + 1, n_q)))

    return pl.pallas_call(
        _regroup_kernel,
        grid=(n_tiles,),
        in_specs=[pl.BlockSpec((D_MODEL, KV_COLS), lambda j: (0, src_tile(j)))],
        out_specs=pl.BlockSpec((D_MODEL, KV_COLS), lambda j: (0, j)),
        out_shape=jax.ShapeDtypeStruct((D_MODEL, D_IN), BF16),
        compiler_params=_params(1),
        name="regroup_w_in",
    )(w)


def _in_proj_kernel(x_ref, g_ref, w_ref, o_ref, kv_ref, u_ref, *, kv_tile, kv_off):
    j = pl.program_id(1)

    @pl.when(j == 0)
    def _():
        u_ref[...] = _rms(x_ref[...], g_ref[...]).astype(BF16)

    acc = _dot(u_ref[...], w_ref[...])
    o_ref[...] = acc.astype(BF16)

    @pl.when(j == kv_tile)
    def _():
        kv_ref[...] = acc[:, kv_off:kv_off + KV_COLS]


def _in_proj(x, g, w, *, tm, tn):
    m = x.shape[0]
    n_tiles = D_IN // tn
    kv_start = D_IN - KV_COLS
    kv_tile, kv_off = kv_start // tn, kv_start % tn
    assert m % tm == 0 and D_IN % tn == 0 and kv_off + KV_COLS <= tn
    return pl.pallas_call(
        functools.partial(_in_proj_kernel, kv_tile=kv_tile, kv_off=kv_off),
        grid=(m // tm, n_tiles),
        in_specs=[
            pl.BlockSpec((tm, D_MODEL), lambda i, j: (i, 0)),
            pl.BlockSpec((1, D_MODEL), lambda i, j: (0, 0)),
            pl.BlockSpec((D_MODEL, tn), lambda i, j: (0, j)),
        ],
        out_specs=[
            pl.BlockSpec((tm, tn), lambda i, j: (i, j)),
            pl.BlockSpec((tm, KV_COLS), lambda i, j: (i, 0)),
        ],
        out_shape=[
            jax.ShapeDtypeStruct((m, D_IN), BF16),
            jax.ShapeDtypeStruct((m, KV_COLS), F32),
        ],
        scratch_shapes=[pltpu.VMEM((tm, D_MODEL), BF16)],
        compiler_params=_params(),
        name="in_proj",
    )(x, g, w)


def _seq_mix_kernel(*refs, t_tok, q_blk, r_chunk, n_t, has_cache):
    if has_cache:
        (vr_ref, gr_ref, qa_ref, qr_ref, kr_ref, kv_ref, ck_ref, cv_ref, st_in_ref, sink_ref,
         a_ref, r_ref, st_out_ref,
         state_ref, kf_ref, dmat_ref, cdec_ref, kdec_ref, bias_ref) = refs
    else:
        (vr_ref, gr_ref, qa_ref, qr_ref, kr_ref, kv_ref, kvh_ref, sink_ref,
         a_ref, r_ref, st_out_ref,
         state_ref, kf_ref, dmat_ref, cdec_ref, kdec_ref, bias_ref) = refs
    b = pl.program_id(0)
    t = pl.program_id(1)
    s_keys = SWA_ROWS + q_blk

    @pl.when((b == 0) & (t == 0))
    def _():
        qi = lax.broadcasted_iota(jnp.int32, (s_keys, q_blk), 1) + SWA_ROWS
        ki = lax.broadcasted_iota(jnp.int32, (s_keys, q_blk), 0)
        dist = jnp.abs(qi - ki).astype(F32)
        qc = qi // CHUNK
        kc = ki // CHUNK
        band = (kc >= qc - SWA_ROWS // CHUNK) & (kc <= qc)
        for hd in range(N_Q_HEADS):
            alibi = -_ALIBI_SLOPES[hd] * dist
            bias_ref[hd] = jnp.where(band, alibi, NEG_INF)
            if not has_cache:
                bias_ref[N_Q_HEADS + hd] = jnp.where(band & (ki >= SWA_ROWS), alibi, NEG_INF)

        diff = (lax.broadcasted_iota(jnp.int32, (r_chunk, r_chunk), 0)
                - lax.broadcasted_iota(jnp.int32, (r_chunk, r_chunk), 1)).astype(F32)
        row_k = lax.broadcasted_iota(jnp.int32, (r_chunk, RET_DK), 0).astype(F32)
        for h in range(N_RET_HEADS):
            lg = _LOG_GAMMA[h]
            dmat_ref[h] = jnp.where(diff >= 0.0, jnp.exp(lg * jnp.maximum(diff, 0.0)), 0.0) * _RET_SCALE
            cdec_ref[h] = jnp.exp(lg * (row_k + 1.0))
            kdec_ref[h] = jnp.exp(lg * (r_chunk - 1.0 - row_k)) * _RET_SCALE

    @pl.when(t == 0)
    def _():
        if has_cache:
            state_ref[...] = st_in_ref[...]
        else:
            state_ref[...] = jnp.zeros_like(state_ref)

    if has_cache:
        kf_ref[0:SWA_ROWS, 0:KV_A_DIM] = ck_ref[...].astype(BF16)
        kf_ref[0:SWA_ROWS, KV_A_DIM:KV_COLS] = cv_ref[...].astype(BF16)
    else:
        kf_ref[0:SWA_ROWS, :] = kvh_ref[...]
    kf_ref[SWA_ROWS:SWA_ROWS + t_tok, :] = kv_ref[...]

    def attention_group(qb, g):
        r0 = qb * q_blk
        first_blk = N_Q_HEADS * jnp.where(t == 0, 1, 0) if (not has_cache and qb == 0) else 0
        k_g = kf_ref[r0:r0 + s_keys, g * HEAD_DIM:(g + 1) * HEAD_DIM]
        v_g = kf_ref[r0:r0 + s_keys, KV_A_DIM + g * HEAD_DIM:KV_A_DIM + (g + 1) * HEAD_DIM]
        q_g = qa_ref[r0:r0 + q_blk, g * GROUP * HEAD_DIM:(g + 1) * GROUP * HEAD_DIM] * jnp.asarray(_ATTN_SCALE, BF16)
        vt_g = v_g.T
        for jh in range(GROUP):
            hd = g * GROUP + jh
            q_h = q_g[:, jh * HEAD_DIM:(jh + 1) * HEAD_DIM]
            s = _dot_nt(k_g, q_h) + bias_ref[hd + first_blk]
            sink = sink_ref[hd]
            m = jnp.maximum(jnp.max(s, axis=0, keepdims=True), sink)
            e = jnp.exp(s - m)
            den = jnp.sum(e, axis=0, keepdims=True) + jnp.exp(sink - m)
            o_t = _dot(vt_g, e.astype(BF16)) * (1.0 / den)
            a_ref[r0:r0 + q_blk, hd * HEAD_DIM:(hd + 1) * HEAD_DIM] = o_t.T.astype(BF16)

    def retention_head(c, h):
        lg = _LOG_GAMMA[h]
        rows = slice(c * r_chunk, (c + 1) * r_chunk)
        q = qr_ref[rows, h * RET_DK:(h + 1) * RET_DK]
        k = kr_ref[rows, h * RET_DK:(h + 1) * RET_DK]
        v = vr_ref[rows, h * RET_DV:(h + 1) * RET_DV]
        st = state_ref[h]
        sc = _dot_nt(q, k) * dmat_ref[h]
        cdec = cdec_ref[h]
        o = _dot(sc.astype(BF16), v) + _dot(q, st.astype(BF16)) * jnp.concatenate([cdec] * (RET_DV // RET_DK), axis=1)
        kd = (k.astype(F32) * kdec_ref[h]).astype(BF16)
        state_ref[h] = st * float(np.exp(np.float32(lg) * np.float32(r_chunk))) + _dot_tn(kd, v)
        c_o = o - jnp.mean(o, axis=-1, keepdims=True)
        nrm = c_o * lax.rsqrt(jnp.mean(c_o * c_o, axis=-1, keepdims=True) + EPS)
        gr = gr_ref[rows, h * RET_DV:(h + 1) * RET_DV].astype(F32)
        r_ref[rows, h * RET_DV:(h + 1) * RET_DV] = (nrm * (gr * _sigmoid(gr))).astype(BF16)

    attn_groups = [(qb, g) for qb in range(t_tok // q_blk) for g in range(N_KV_HEADS)]
    ret_heads = [(c, h) for c in range(t_tok // r_chunk) for h in range(N_RET_HEADS)]
    for i in range(max(len(attn_groups), len(ret_heads))):
        if i < len(ret_heads):
            retention_head(*ret_heads[i])
        if i < len(attn_groups):
            attention_group(*attn_groups[i])

    @pl.when(t == n_t - 1)
    def _():
        st_out_ref[...] = state_ref[...]


def _seq_mix(proj, sinks, *, n_b, n_t, t_tok, q_blk, r_chunk, cache=None):
    has_cache = cache is not None
    assert t_tok % q_blk == 0 and t_tok % r_chunk == 0 and (has_cache or q_blk == SWA_ROWS)
    row = lambda b, t: b * n_t + t
    halo_per_tile = t_tok // SWA_ROWS if not has_cache else 0
    in_specs = [
        pl.BlockSpec((t_tok, RET_V_DIM), lambda b, t: (row(b, t), COL_VR)),
        pl.BlockSpec((t_tok, RET_V_DIM), lambda b, t: (row(b, t), COL_GR)),
        pl.BlockSpec((t_tok, Q_A_DIM), lambda b, t: (row(b, t), COL_QA)),
        pl.BlockSpec((t_tok, RET_QK_DIM), lambda b, t: (row(b, t), COL_QR)),
        pl.BlockSpec((t_tok, RET_QK_DIM), lambda b, t: (row(b, t), COL_KR)),
        pl.BlockSpec((t_tok, KV_COLS), lambda b, t: (row(b, t), COL_KV)),
    ]
    args = [proj] * 6
    if has_cache:
        ck, cv, st_in = cache
        in_specs += [
            pl.BlockSpec((None, SWA_ROWS, KV_A_DIM), lambda b, t: (b, 0, 0)),
            pl.BlockSpec((None, SWA_ROWS, KV_A_DIM), lambda b, t: (b, 0, 0)),
            pl.BlockSpec((None, N_RET_HEADS, RET_DK, RET_DV), lambda b, t: (b, 0, 0, 0)),
        ]
        args += [ck, cv, st_in]
    else:
        in_specs.append(pl.BlockSpec(
            (SWA_ROWS, KV_COLS), lambda b, t: (jnp.maximum(row(b, t) * halo_per_tile - 1, 0), COL_KV)))
        args.append(proj)
    in_specs.append(pl.BlockSpec(memory_space=pltpu.SMEM))
    args.append(sinks)
    m = n_b * n_t * t_tok
    return pl.pallas_call(
        functools.partial(_seq_mix_kernel, t_tok=t_tok, q_blk=q_blk, r_chunk=r_chunk, n_t=n_t, has_cache=has_cache),
        grid=(n_b, n_t),
        in_specs=in_specs,
        out_specs=[
            pl.BlockSpec((t_tok, Q_A_DIM), lambda b, t: (row(b, t), 0)),
            pl.BlockSpec((t_tok, RET_V_DIM), lambda b, t: (row(b, t), 0)),
            pl.BlockSpec((None, N_RET_HEADS, RET_DK, RET_DV), lambda b, t: (b, 0, 0, 0)),
        ],
        out_shape=[
            jax.ShapeDtypeStruct((m, Q_A_DIM), BF16),
            jax.ShapeDtypeStruct((m, RET_V_DIM), BF16),
            jax.ShapeDtypeStruct((n_b, N_RET_HEADS, RET_DK, RET_DV), F32),
        ],
        scratch_shapes=[
            pltpu.VMEM((N_RET_HEADS, RET_DK, RET_DV), F32),
            pltpu.VMEM((SWA_ROWS + t_tok, KV_COLS), BF16),
            pltpu.VMEM((N_RET_HEADS, r_chunk, r_chunk), F32),
            pltpu.VMEM((N_RET_HEADS, r_chunk, RET_DK), F32),
            pltpu.VMEM((N_RET_HEADS, r_chunk, RET_DK), F32),
            pltpu.VMEM(((1 if has_cache else 2) * N_Q_HEADS, SWA_ROWS + q_blk, q_blk), F32),
        ],
        compiler_params=_params(),
        name="seq_mix_sample" if has_cache else "seq_mix_prompt",
    )(*args)


def _merge_kernel(x_ref, a_ref, r_ref, ga_ref, gb_ref, wau_ref, wru_ref, wout_ref, gpost_ref, h_ref):
    merged = (_sigmoid(ga_ref[...].astype(F32)) * _dot(a_ref[...], wau_ref[...])
              + _sigmoid(gb_ref[...].astype(F32)) * _dot(r_ref[...], wru_ref[...]))
    mo = _dot(merged.astype(BF16), wout_ref[...])
    h_ref[...] = x_ref[...] + _rms(mo, gpost_ref[...])


def _merge(x, a, r, proj, wau, wru, wout, gpost, *, tm):
    m = x.shape[0]
    assert m % tm == 0
    const = lambda shape: pl.BlockSpec(shape, lambda i: (0,) * len(shape), pipeline_mode=pl.Buffered(1))
    return pl.pallas_call(
        _merge_kernel,
        grid=(m // tm,),
        in_specs=[
            pl.BlockSpec((tm, D_MODEL), lambda i: (i, 0)),
            pl.BlockSpec((tm, Q_A_DIM), lambda i: (i, 0)),
            pl.BlockSpec((tm, RET_V_DIM), lambda i: (i, 0)),
            pl.BlockSpec((tm, D_MODEL), lambda i: (i, COL_GA)),
            pl.BlockSpec((tm, D_MODEL), lambda i: (i, COL_GB)),
            const((Q_A_DIM, D_MODEL)),
            const((RET_V_DIM, D_MODEL)),
            const((D_MODEL, D_MODEL)),
            const((1, D_MODEL)),
        ],
        out_specs=pl.BlockSpec((tm, D_MODEL), lambda i: (i, 0)),
        out_shape=jax.ShapeDtypeStruct((m, D_MODEL), F32),
        compiler_params=_params(1),
        name="merge",
    )(x, a, r, proj, proj, wau, wru, wout, gpost)


def _ffn_kernel(h_ref, p_ref, wg_ref, wu_ref, wo_ref, gpre_ref, gpost_ref, gple_ref, wpg_ref, wple_ref,
                y_ref, t_ref, acc_ref, *, n_f):
    j = pl.program_id(1)

    @pl.when(j == 0)
    def _():
        t_ref[...] = _rms(h_ref[...], gpre_ref[...]).astype(BF16)
        acc_ref[...] = jnp.zeros_like(acc_ref)

    tt = t_ref[...]
    g = _dot(tt, wg_ref[...])
    up = _dot(tt, wu_ref[...])
    act = (g * _sigmoid(g) * up).astype(BF16)
    acc_ref[...] += _dot(act, wo_ref[...])

    @pl.when(j == n_f - 1)
    def _():
        h2 = h_ref[...] + _rms(acc_ref[...], gpost_ref[...])
        z = _rms(h2, gple_ref[...]).astype(BF16)
        gate = _sigmoid(_dot(z, wpg_ref[...]))
        y_ref[...] = h2 + gate * _dot(p_ref[...].astype(BF16), wple_ref[...])


def _ffn(h, ple, wff_in, wff_out, gpre, gpost, gple, wpg, wple, *, tm, tf):
    m = h.shape[0]
    n_f = D_FF // tf
    assert m % tm == 0 and D_FF % tf == 0
    const = lambda shape: pl.BlockSpec(shape, lambda i, j: (0,) * len(shape), pipeline_mode=pl.Buffered(1))
    return pl.pallas_call(
        functools.partial(_ffn_kernel, n_f=n_f),
        grid=(m // tm, n_f),
        in_specs=[
            pl.BlockSpec((tm, D_MODEL), lambda i, j: (i, 0)),
            pl.BlockSpec((tm, PLE_DIM), lambda i, j: (i, 0)),
            pl.BlockSpec((D_MODEL, tf), lambda i, j: (0, j)),
            pl.BlockSpec((D_MODEL, tf), lambda i, j: (0, j + n_f)),
            pl.BlockSpec((tf, D_MODEL), lambda i, j: (j, 0)),
            const((1, D_MODEL)),
            const((1, D_MODEL)),
            const((1, D_MODEL)),
            const((D_MODEL, D_MODEL)),
            const((PLE_DIM, D_MODEL)),
        ],
        out_specs=pl.BlockSpec((tm, D_MODEL), lambda i, j: (i, 0)),
        out_shape=jax.ShapeDtypeStruct((m, D_MODEL), F32),
        scratch_shapes=[pltpu.VMEM((tm, D_MODEL), BF16), pltpu.VMEM((tm, D_MODEL), F32)],
        compiler_params=_params(),
        name="ffn_ple",
    )(h, ple, wff_in, wff_in, wff_out, gpre, gpost, gple, wpg, wple)


def _layer(x, ple, w, *, n_b, seq, cache=None):
    proj, kv = _in_proj(x, w["g_pre_mix"], w["w_in"], tm=512, tn=D_IN // 4)
    if cache is None:
        a, r, state = _seq_mix(proj, w["sinks"], n_b=n_b, n_t=seq // 512, t_tok=512, q_blk=SWA_ROWS, r_chunk=256)
    else:
        a, r, state = _seq_mix(proj, w["sinks"], n_b=n_b, n_t=1, t_tok=seq, q_blk=seq, r_chunk=seq, cache=cache)
    h = _merge(x, a, r, proj, w["w_a_up"], w["w_r_up"], w["w_out"], w["g_post_mix"], tm=512)
    y = _ffn(h, ple, w["w_ffn_in"], w["w_ffn_out"], w["g_pre_ffn"], w["g_post_ffn"], w["g_ple"],
             w["w_ple_gate"], w["w_ple"], tm=512, tf=512)
    kv = kv.reshape(n_b, seq, KV_COLS)[:, max(seq - SWA_ROWS, 0):]
    return y, kv[:, :, :KV_A_DIM], kv[:, :, KV_A_DIM:], state


def kernel(x_prompt, x_sample, cache_swa_k, cache_swa_v, state_ret, p_prompt, p_sample, norm_pre_mix, w_in, attn_sinks, w_a_up, w_r_up, w_out, norm_post_mix, norm_pre_ffn, w_ffn_in, w_ffn_out, norm_post_ffn, norm_ple, w_ple_gate, w_ple):
    depth = w_in.shape[0]
    n_bp, seq_p, _ = x_prompt.shape
    n_bs, seq_s, _ = x_sample.shape
    hp = x_prompt.reshape(n_bp * seq_p, D_MODEL)
    hs = x_sample.reshape(n_bs * seq_s, D_MODEL)
    at = (lambda a, l: a.reshape(a.shape[1:])) if depth == 1 else (lambda a, l: a[l])
    outs = [[] for _ in range(6)]
    for l in range(depth):
        w = {
            "w_in": _regroup_w_in(at(w_in, l)),
            "w_a_up": at(w_a_up, l).astype(BF16),
            "w_r_up": at(w_r_up, l).astype(BF16),
            "w_out": at(w_out, l).astype(BF16),
            "w_ffn_in": at(w_ffn_in, l).astype(BF16),
            "w_ffn_out": at(w_ffn_out, l).astype(BF16),
            "w_ple_gate": at(w_ple_gate, l).astype(BF16),
            "w_ple": at(w_ple, l).astype(BF16),
            "sinks": at(attn_sinks, l),
            "g_pre_mix": at(norm_pre_mix, l).reshape(1, D_MODEL),
            "g_post_mix": at(norm_post_mix, l).reshape(1, D_MODEL),
            "g_pre_ffn": at(norm_pre_ffn, l).reshape(1, D_MODEL),
            "g_post_ffn": at(norm_post_ffn, l).reshape(1, D_MODEL),
            "g_ple": at(norm_ple, l).reshape(1, D_MODEL),
        }
        hp, kp, vp, sp = _layer(hp, at(p_prompt, l).reshape(n_bp * seq_p, PLE_DIM), w, n_b=n_bp, seq=seq_p)
        cache = (at(cache_swa_k, l).reshape(n_bs, SWA_ROWS, KV_A_DIM), at(cache_swa_v, l).reshape(n_bs, SWA_ROWS, KV_A_DIM),
                 at(state_ret, l))
        hs, k_s, v_s, s_s = _layer(hs, at(p_sample, l).reshape(n_bs * seq_s, PLE_DIM), w, n_b=n_bs, seq=seq_s, cache=cache)
        kv_shape = (SWA_ROWS, N_KV_HEADS, HEAD_DIM)
        outs[0].append(kp.reshape(n_bp, *kv_shape))
        outs[1].append(vp.reshape(n_bp, *kv_shape))
        outs[2].append(sp)
        outs[3].append(jnp.concatenate([cache[0][:, seq_s:], k_s], axis=1).reshape(n_bs, *kv_shape))
        outs[4].append(jnp.concatenate([cache[1][:, seq_s:], v_s], axis=1).reshape(n_bs, *kv_shape))
        outs[5].append(s_s)
    stack = (lambda o: o[0][None]) if depth == 1 else jnp.stack
    return (hp.reshape(n_bp, seq_p, D_MODEL), hs.reshape(n_bs, seq_s, D_MODEL), *(stack(o) for o in outs))
```

```python
import functools

import jax
import jax.numpy as jnp
import numpy as np
from jax import lax
from jax.experimental import pallas as pl
from jax.experimental.pallas import tpu as pltpu

F32 = jnp.float32
BF16 = jnp.bfloat16

D_MODEL = 2048
CHUNK = 64
PLE_DIM = 256
N_Q_HEADS = 16
N_KV_HEADS = 4
GROUP = N_Q_HEADS // N_KV_HEADS
HEAD_DIM = 64
SWA_ROWS = 128
Q_A_DIM = N_Q_HEADS * HEAD_DIM
KV_A_DIM = N_KV_HEADS * HEAD_DIM
N_RET_HEADS = 8
RET_DK = 128
RET_DV = 256
RET_QK_DIM = N_RET_HEADS * RET_DK
RET_V_DIM = N_RET_HEADS * RET_DV
D_FF = ((8 * D_MODEL // 3 + 255) // 256) * 256
D_IN = Q_A_DIM + 2 * KV_A_DIM + 2 * RET_QK_DIM + 2 * RET_V_DIM + 2 * D_MODEL
NEG_INF = -1e30
EPS = 1e-6

KV_COLS = 2 * KV_A_DIM
PROJ_A_COLS = 2 * RET_V_DIM + D_MODEL
PROJ_B_COLS = D_IN - PROJ_A_COLS
A_VR, A_GR, A_GA = 0, 1, 2
B_GB = 0
B_QA, B_QR, B_KR = 2, 3, 4
B_KV = (PROJ_B_COLS - KV_COLS) // KV_COLS

V7X_VMEM_LIMIT_BYTES = 60 * 1024 * 1024

_ALIBI_SLOPES = [float(v) for v in np.exp2(np.float32(-8.0) * (np.arange(N_Q_HEADS, dtype=np.float32) + 1.0) / N_Q_HEADS)]
_LOG_GAMMA = [float(v) for v in np.log(np.float32(1.0) - np.exp2(np.float32(-5.0) - np.arange(N_RET_HEADS, dtype=np.float32)))]
_ATTN_SCALE = HEAD_DIM ** -0.5
_RET_SCALE = RET_DK ** -0.5


def _dot(a, b):
    return jnp.dot(a, b, preferred_element_type=F32)


def _dot_nt(a, b):
    return lax.dot_general(a, b, (((1,), (1,)), ((), ())), preferred_element_type=F32)


def _dot_tn(a, b):
    return lax.dot_general(a, b, (((0,), (0,)), ((), ())), preferred_element_type=F32)


def _rms(x, g):
    return x * lax.rsqrt(jnp.mean(x * x, axis=-1, keepdims=True) + EPS) * g


def _sigmoid(x):
    return 1.0 / (1.0 + jnp.exp(-x))


def _params(grid_rank=2):
    return pltpu.CompilerParams(dimension_semantics=("arbitrary",) * grid_rank,
                                vmem_limit_bytes=V7X_VMEM_LIMIT_BYTES)


def _regroup_kernel(w_ref, oa_ref, ob_ref, *, n_a):
    j = pl.program_id(0)

    @pl.when(j < n_a)
    def _():
        oa_ref[...] = w_ref[...].astype(BF16)

    @pl.when(j >= n_a)
    def _():
        ob_ref[...] = w_ref[...].astype(BF16)


def _regroup_w_in(w):
    n_tiles = D_IN // KV_COLS
    n_a = PROJ_A_COLS // KV_COLS
    ret_v_tile = (Q_A_DIM + KV_COLS + 2 * RET_QK_DIM) // KV_COLS
    n_wide = n_tiles - ret_v_tile
    n_q = Q_A_DIM // KV_COLS

    def src_tile(j):
        return jnp.where(j < n_wide, j + ret_v_tile,
                         jnp.where(j < n_wide + n_q, j - n_wide,
                                   jnp.where(j < n_tiles - 1, j - n_wide + 1, n_q)))

    return pl.pallas_call(
        functools.partial(_regroup_kernel, n_a=n_a),
        grid=(n_tiles,),
        in_specs=[pl.BlockSpec((D_MODEL, KV_COLS), lambda j: (0, src_tile(j)))],
        out_specs=[
            pl.BlockSpec((D_MODEL, KV_COLS), lambda j: (0, jnp.minimum(j, n_a - 1))),
            pl.BlockSpec((D_MODEL, KV_COLS), lambda j: (0, jnp.maximum(j - n_a, 0))),
        ],
        out_shape=[
            jax.ShapeDtypeStruct((D_MODEL, PROJ_A_COLS), BF16),
            jax.ShapeDtypeStruct((D_MODEL, PROJ_B_COLS), BF16),
        ],
        compiler_params=_params(1),
        name="regroup_w_in",
    )(w)


def _in_proj_a_kernel(x_ref, g_ref, w_ref, o_ref, u_ref):
    @pl.when(pl.program_id(1) == 0)
    def _():
        u_ref[...] = _rms(x_ref[...], g_ref[...]).astype(BF16)

    o_ref[...] = _dot(u_ref[...], w_ref[...]).astype(BF16)


def _in_proj_b_kernel(u_ref, w_ref, o_ref, kv_ref, *, kv_tile, kv_off):
    acc = _dot(u_ref[...], w_ref[...])
    o_ref[...] = acc.astype(BF16)

    @pl.when(pl.program_id(1) == kv_tile)
    def _():
        kv_ref[...] = acc[:, kv_off:kv_off + KV_COLS]


def _in_proj(x, g, w_a, w_b, *, tm_a, tm_b, n_tiles):
    m = x.shape[0]
    tn_a, tn_b = PROJ_A_COLS // n_tiles, PROJ_B_COLS // n_tiles
    kv_start = PROJ_B_COLS - KV_COLS
    kv_tile, kv_off = kv_start // tn_b, kv_start % tn_b
    assert m % tm_a == 0 and m % tm_b == 0 and PROJ_A_COLS % n_tiles == 0 and PROJ_B_COLS % n_tiles == 0
    assert kv_off + KV_COLS <= tn_b
    proj_a, u = pl.pallas_call(
        _in_proj_a_kernel,
        grid=(m // tm_a, n_tiles),
        in_specs=[
            pl.BlockSpec((tm_a, D_MODEL), lambda i, j: (i, 0)),
            pl.BlockSpec((1, D_MODEL), lambda i, j: (0, 0)),
            pl.BlockSpec((D_MODEL, tn_a), lambda i, j: (0, j)),
        ],
        out_specs=[
            pl.BlockSpec((tm_a, tn_a), lambda i, j: (i, j)),
            pl.BlockSpec((tm_a, D_MODEL), lambda i, j: (i, 0)),
        ],
        out_shape=[
            jax.ShapeDtypeStruct((m, PROJ_A_COLS), BF16),
            jax.ShapeDtypeStruct((m, D_MODEL), BF16),
        ],
        compiler_params=_params(),
        name="in_proj_a",
    )(x, g, w_a)
    proj_b, kv = pl.pallas_call(
        functools.partial(_in_proj_b_kernel, kv_tile=kv_tile, kv_off=kv_off),
        grid=(m // tm_b, n_tiles),
        in_specs=[
            pl.BlockSpec((tm_b, D_MODEL), lambda i, j: (i, 0)),
            pl.BlockSpec((D_MODEL, tn_b), lambda i, j: (0, j)),
        ],
        out_specs=[
            pl.BlockSpec((tm_b, tn_b), lambda i, j: (i, j)),
            pl.BlockSpec((tm_b, KV_COLS), lambda i, j: (i, 0)),
        ],
        out_shape=[
            jax.ShapeDtypeStruct((m, PROJ_B_COLS), BF16),
            jax.ShapeDtypeStruct((m, KV_COLS), F32),
        ],
        compiler_params=_params(),
        name="in_proj_b",
    )(u, w_b)
    return proj_a, proj_b, kv


def _fill_tables(bias_ref, pen_ref, dmat_ref, cdec_ref, kdec_ref, *, q_blk, r_chunk):
    s_keys = SWA_ROWS + q_blk
    qi = lax.broadcasted_iota(jnp.int32, (s_keys, q_blk), 1) + SWA_ROWS
    ki = lax.broadcasted_iota(jnp.int32, (s_keys, q_blk), 0)
    dist = jnp.abs(qi - ki).astype(F32)
    qc = qi // CHUNK
    kc = ki // CHUNK
    band = (kc >= qc - SWA_ROWS // CHUNK) & (kc <= qc)
    for hd in range(N_Q_HEADS):
        bias_ref[hd] = jnp.where(band, -_ALIBI_SLOPES[hd] * dist, NEG_INF)
    if pen_ref is not None:
        pen_ref[0] = jnp.zeros((s_keys, q_blk), F32)
        pen_ref[1] = jnp.where(ki >= SWA_ROWS, 0.0, NEG_INF)

    diff = (lax.broadcasted_iota(jnp.int32, (r_chunk, r_chunk), 0)
            - lax.broadcasted_iota(jnp.int32, (r_chunk, r_chunk), 1)).astype(F32)
    row_k = lax.broadcasted_iota(jnp.int32, (r_chunk, RET_DK), 0).astype(F32)
    for h in range(N_RET_HEADS):
        lg = _LOG_GAMMA[h]
        dmat_ref[h] = jnp.where(diff >= 0.0, jnp.exp(lg * jnp.maximum(diff, 0.0)), 0.0) * _RET_SCALE
        cdec_ref[h] = jnp.exp(lg * (row_k + 1.0))
        kdec_ref[h] = jnp.exp(lg * (r_chunk - 1.0 - row_k)) * _RET_SCALE


def _seq_mix_items(*, vr_ref, gr_ref, qa_ref, qr_ref, kr_ref, kf_ref, sink_ref, a_ref, r_ref, state_ref,
                   bias_ref, pen_ref, dmat_ref, cdec_ref, kdec_ref, first_tile, t_tok, q_blk, r_chunk):
    s_keys = SWA_ROWS + q_blk

    def attention_group(qb, g):
        r0 = qb * q_blk
        k_g = kf_ref[r0:r0 + s_keys, g * HEAD_DIM:(g + 1) * HEAD_DIM]
        v_g = kf_ref[r0:r0 + s_keys, KV_A_DIM + g * HEAD_DIM:KV_A_DIM + (g + 1) * HEAD_DIM]
        q_g = qa_ref[r0:r0 + q_blk, g * GROUP * HEAD_DIM:(g + 1) * GROUP * HEAD_DIM] * jnp.asarray(_ATTN_SCALE, BF16)
        vt_g = v_g.T
        for jh in range(GROUP):
            hd = g * GROUP + jh
            q_h = q_g[:, jh * HEAD_DIM:(jh + 1) * HEAD_DIM]
            s = _dot_nt(k_g, q_h) + bias_ref[hd]
            if pen_ref is not None and qb == 0:
                s = s + pen_ref[first_tile]
            sink = sink_ref[hd]
            m = jnp.maximum(jnp.max(s, axis=0, keepdims=True), sink)
            e = jnp.exp(s - m)
            den = jnp.sum(e, axis=0, keepdims=True) + jnp.exp(sink - m)
            o_t = _dot(vt_g, e.astype(BF16)) * (1.0 / den)
            a_ref[r0:r0 + q_blk, hd * HEAD_DIM:(hd + 1) * HEAD_DIM] = o_t.T.astype(BF16)

    def retention_head(c, h):
        lg = _LOG_GAMMA[h]
        rows = slice(c * r_chunk, (c + 1) * r_chunk)
        q = qr_ref[rows, h * RET_DK:(h + 1) * RET_DK]
        k = kr_ref[rows, h * RET_DK:(h + 1) * RET_DK]
        v = vr_ref[rows, h * RET_DV:(h + 1) * RET_DV]
        st = state_ref[h]
        sc = _dot_nt(q, k) * dmat_ref[h]
        cdec = cdec_ref[h]
        o = _dot(sc.astype(BF16), v) + _dot(q, st.astype(BF16)) * jnp.concatenate([cdec] * (RET_DV // RET_DK), axis=1)
        kd = (k.astype(F32) * kdec_ref[h]).astype(BF16)
        state_ref[h] = st * float(np.exp(np.float32(lg) * np.float32(r_chunk))) + _dot_tn(kd, v)
        c_o = o - jnp.mean(o, axis=-1, keepdims=True)
        nrm = c_o * lax.rsqrt(jnp.mean(c_o * c_o, axis=-1, keepdims=True) + EPS)
        gr = gr_ref[rows, h * RET_DV:(h + 1) * RET_DV].astype(F32)
        r_ref[rows, h * RET_DV:(h + 1) * RET_DV] = (nrm * (gr * _sigmoid(gr))).astype(BF16)

    attn_groups = [(qb, g) for qb in range(t_tok // q_blk) for g in range(N_KV_HEADS)]
    ret_heads = [(c, h) for c in range(t_tok // r_chunk) for h in range(N_RET_HEADS)]
    items = []
    for i in range(max(len(attn_groups), len(ret_heads))):
        if i < len(ret_heads):
            items.append(functools.partial(retention_head, *ret_heads[i]))
        if i < len(attn_groups):
            items.append(functools.partial(attention_group, *attn_groups[i]))
    return items


def _seq_scratch(t_tok, q_blk, r_chunk, with_pen):
    shapes = [
        pltpu.VMEM((N_RET_HEADS, RET_DK, RET_DV), F32),
        pltpu.VMEM((SWA_ROWS + t_tok, KV_COLS), BF16),
        pltpu.VMEM((N_Q_HEADS, SWA_ROWS + q_blk, q_blk), F32),
        pltpu.VMEM((N_RET_HEADS, r_chunk, r_chunk), F32),
        pltpu.VMEM((N_RET_HEADS, r_chunk, RET_DK), F32),
        pltpu.VMEM((N_RET_HEADS, r_chunk, RET_DK), F32),
    ]
    if with_pen:
        shapes.append(pltpu.VMEM((2, SWA_ROWS + q_blk, q_blk), F32))
    return shapes


def _seq_mix_kernel(vr_ref, gr_ref, qa_ref, qr_ref, kr_ref, kv_ref, ck_ref, cv_ref, st_in_ref, sink_ref,
                    a_ref, r_ref, st_out_ref,
                    state_ref, kf_ref, bias_ref, dmat_ref, cdec_ref, kdec_ref, *, t_tok):
    @pl.when(pl.program_id(0) == 0)
    def _():
        _fill_tables(bias_ref, None, dmat_ref, cdec_ref, kdec_ref, q_blk=t_tok, r_chunk=t_tok)

    state_ref[...] = st_in_ref[...]
    kf_ref[0:SWA_ROWS, 0:KV_A_DIM] = ck_ref[...].astype(BF16)
    kf_ref[0:SWA_ROWS, KV_A_DIM:KV_COLS] = cv_ref[...].astype(BF16)
    kf_ref[SWA_ROWS:SWA_ROWS + t_tok, :] = kv_ref[...]
    for item in _seq_mix_items(
            vr_ref=vr_ref, gr_ref=gr_ref, qa_ref=qa_ref, qr_ref=qr_ref, kr_ref=kr_ref, kf_ref=kf_ref,
            sink_ref=sink_ref, a_ref=a_ref, r_ref=r_ref, state_ref=state_ref, bias_ref=bias_ref, pen_ref=None,
            dmat_ref=dmat_ref, cdec_ref=cdec_ref, kdec_ref=kdec_ref, first_tile=None,
            t_tok=t_tok, q_blk=t_tok, r_chunk=t_tok):
        item()
    st_out_ref[...] = state_ref[...]


def _seq_mix(proj_a, proj_b, sinks, cache, *, n_b, t_tok):
    ck, cv, st_in = cache
    m = n_b * t_tok
    return pl.pallas_call(
        functools.partial(_seq_mix_kernel, t_tok=t_tok),
        grid=(n_b,),
        in_specs=[
            pl.BlockSpec((t_tok, RET_V_DIM), lambda b: (b, A_VR)),
            pl.BlockSpec((t_tok, RET_V_DIM), lambda b: (b, A_GR)),
            pl.BlockSpec((t_tok, Q_A_DIM), lambda b: (b, B_QA)),
            pl.BlockSpec((t_tok, RET_QK_DIM), lambda b: (b, B_QR)),
            pl.BlockSpec((t_tok, RET_QK_DIM), lambda b: (b, B_KR)),
            pl.BlockSpec((t_tok, KV_COLS), lambda b: (b, B_KV)),
            pl.BlockSpec((None, SWA_ROWS, KV_A_DIM), lambda b: (b, 0, 0)),
            pl.BlockSpec((None, SWA_ROWS, KV_A_DIM), lambda b: (b, 0, 0)),
            pl.BlockSpec((None, N_RET_HEADS, RET_DK, RET_DV), lambda b: (b, 0, 0, 0)),
            pl.BlockSpec(memory_space=pltpu.SMEM),
        ],
        out_specs=[
            pl.BlockSpec((t_tok, Q_A_DIM), lambda b: (b, 0)),
            pl.BlockSpec((t_tok, RET_V_DIM), lambda b: (b, 0)),
            pl.BlockSpec((None, N_RET_HEADS, RET_DK, RET_DV), lambda b: (b, 0, 0, 0)),
        ],
        out_shape=[
            jax.ShapeDtypeStruct((m, Q_A_DIM), BF16),
            jax.ShapeDtypeStruct((m, RET_V_DIM), BF16),
            jax.ShapeDtypeStruct((n_b, N_RET_HEADS, RET_DK, RET_DV), F32),
        ],
        scratch_shapes=_seq_scratch(t_tok, t_tok, t_tok, with_pen=False),
        compiler_params=_params(1),
        name="seq_mix_sample",
    )(proj_a, proj_a, proj_b, proj_b, proj_b, proj_b, ck, cv, st_in, sinks)


def _merge_math(x, a, r, ga, gb, wau_ref, wru_ref, wout_ref, gpost_ref):
    merged = (_sigmoid(ga.astype(F32)) * _dot(a, wau_ref[...])
              + _sigmoid(gb.astype(F32)) * _dot(r, wru_ref[...]))
    mo = _dot(merged.astype(BF16), wout_ref[...])
    return x + _rms(mo, gpost_ref[...])


def _merge_kernel(x_ref, a_ref, r_ref, ga_ref, gb_ref, wau_ref, wru_ref, wout_ref, gpost_ref, h_ref):
    h_ref[...] = _merge_math(x_ref[...], a_ref[...], r_ref[...], ga_ref[...], gb_ref[...],
                             wau_ref, wru_ref, wout_ref, gpost_ref)


def _merge(x, a, r, proj_a, proj_b, wau, wru, wout, gpost, *, tm):
    m = x.shape[0]
    assert m % tm == 0
    const = lambda shape: pl.BlockSpec(shape, lambda i: (0,) * len(shape), pipeline_mode=pl.Buffered(1))
    return pl.pallas_call(
        _merge_kernel,
        grid=(m // tm,),
        in_specs=[
            pl.BlockSpec((tm, D_MODEL), lambda i: (i, 0)),
            pl.BlockSpec((tm, Q_A_DIM), lambda i: (i, 0)),
            pl.BlockSpec((tm, RET_V_DIM), lambda i: (i, 0)),
            pl.BlockSpec((tm, D_MODEL), lambda i: (i, A_GA)),
            pl.BlockSpec((tm, D_MODEL), lambda i: (i, B_GB)),
            const((Q_A_DIM, D_MODEL)),
            const((RET_V_DIM, D_MODEL)),
            const((D_MODEL, D_MODEL)),
            const((1, D_MODEL)),
        ],
        out_specs=pl.BlockSpec((tm, D_MODEL), lambda i: (i, 0)),
        out_shape=jax.ShapeDtypeStruct((m, D_MODEL), F32),
        compiler_params=_params(1),
        name="merge",
    )(x, a, r, proj_a, proj_b, wau, wru, wout, gpost)


def _mixer_kernel(vr_ref, gr_ref, qa_ref, qr_ref, kr_ref, kv_ref, kvh_ref, sink_ref,
                  x_ref, ga_ref, gb_ref, wau_ref, wru_ref, wout_ref, gpost_ref,
                  h_ref, st_out_ref,
                  state_ref, kf_ref, bias_ref, dmat_ref, cdec_ref, kdec_ref, pen_ref, a_scr, r_scr,
                  *, t_tok, q_blk, r_chunk, n_t, n_tiles):
    s = pl.program_id(0)
    t = lax.rem(jnp.minimum(s, n_tiles - 1), n_t)

    @pl.when(s == 0)
    def _():
        _fill_tables(bias_ref, pen_ref, dmat_ref, cdec_ref, kdec_ref, q_blk=q_blk, r_chunk=r_chunk)
        a_scr[1] = jnp.zeros(a_scr.shape[1:], BF16)
        r_scr[1] = jnp.zeros(r_scr.shape[1:], BF16)

    @pl.when(t == 0)
    def _():
        state_ref[...] = jnp.zeros_like(state_ref)

    kf_ref[0:SWA_ROWS, :] = kvh_ref[...]
    kf_ref[SWA_ROWS:SWA_ROWS + t_tok, :] = kv_ref[...]
    first_tile = jnp.where(t == 0, 1, 0)

    def step(w, r):
        h_ref[...] = _merge_math(x_ref[...], a_scr[r], r_scr[r], ga_ref[...], gb_ref[...],
                                 wau_ref, wru_ref, wout_ref, gpost_ref)
        for item in _seq_mix_items(
                vr_ref=vr_ref, gr_ref=gr_ref, qa_ref=qa_ref, qr_ref=qr_ref, kr_ref=kr_ref, kf_ref=kf_ref,
                sink_ref=sink_ref, a_ref=a_scr.at[w], r_ref=r_scr.at[w], state_ref=state_ref, bias_ref=bias_ref,
                pen_ref=pen_ref, dmat_ref=dmat_ref, cdec_ref=cdec_ref, kdec_ref=kdec_ref, first_tile=first_tile,
                t_tok=t_tok, q_blk=q_blk, r_chunk=r_chunk):
            item()

    parity = lax.rem(s, 2)
    pl.when(parity == 0)(functools.partial(step, 0, 1))
    pl.when(parity == 1)(functools.partial(step, 1, 0))

    @pl.when((t == n_t - 1) & (s < n_tiles))
    def _():
        st_out_ref[...] = state_ref[...]


def _mixer(x, proj_a, proj_b, sinks, wau, wru, wout, gpost, *, n_b, n_t, t_tok, q_blk, r_chunk):
    assert t_tok % q_blk == 0 and t_tok % r_chunk == 0 and q_blk == SWA_ROWS
    n_tiles = n_b * n_t
    halo_per_tile = t_tok // SWA_ROWS
    seq_tile = lambda s: jnp.minimum(s, n_tiles - 1)
    mrg_tile = lambda s: jnp.maximum(s - 1, 0)
    const = lambda shape: pl.BlockSpec(shape, lambda s: (0,) * len(shape), pipeline_mode=pl.Buffered(1))
    return pl.pallas_call(
        functools.partial(_mixer_kernel, t_tok=t_tok, q_blk=q_blk, r_chunk=r_chunk, n_t=n_t, n_tiles=n_tiles),
        grid=(n_tiles + 1,),
        in_specs=[
            pl.BlockSpec((t_tok, RET_V_DIM), lambda s: (seq_tile(s), A_VR)),
            pl.BlockSpec((t_tok, RET_V_DIM), lambda s: (seq_tile(s), A_GR)),
            pl.BlockSpec((t_tok, Q_A_DIM), lambda s: (seq_tile(s), B_QA)),
            pl.BlockSpec((t_tok, RET_QK_DIM), lambda s: (seq_tile(s), B_QR)),
            pl.BlockSpec((t_tok, RET_QK_DIM), lambda s: (seq_tile(s), B_KR)),
            pl.BlockSpec((t_tok, KV_COLS), lambda s: (seq_tile(s), B_KV)),
            pl.BlockSpec((SWA_ROWS, KV_COLS), lambda s: (jnp.maximum(seq_tile(s) * halo_per_tile - 1, 0), B_KV)),
            pl.BlockSpec(memory_space=pltpu.SMEM),
            pl.BlockSpec((t_tok, D_MODEL), lambda s: (mrg_tile(s), 0)),
            pl.BlockSpec((t_tok, D_MODEL), lambda s: (mrg_tile(s), A_GA)),
            pl.BlockSpec((t_tok, D_MODEL), lambda s: (mrg_tile(s), B_GB)),
            const((Q_A_DIM, D_MODEL)),
            const((RET_V_DIM, D_MODEL)),
            const((D_MODEL, D_MODEL)),
            const((1, D_MODEL)),
        ],
        out_specs=[
            pl.BlockSpec((t_tok, D_MODEL), lambda s: (mrg_tile(s), 0)),
            pl.BlockSpec((None, N_RET_HEADS, RET_DK, RET_DV), lambda s: (seq_tile(s) // n_t, 0, 0, 0)),
        ],
        out_shape=[
            jax.ShapeDtypeStruct((n_tiles * t_tok, D_MODEL), F32),
            jax.ShapeDtypeStruct((n_b, N_RET_HEADS, RET_DK, RET_DV), F32),
        ],
        scratch_shapes=_seq_scratch(t_tok, q_blk, r_chunk, with_pen=True) + [
            pltpu.VMEM((2, t_tok, Q_A_DIM), BF16),
            pltpu.VMEM((2, t_tok, RET_V_DIM), BF16),
        ],
        compiler_params=_params(1),
        name="mixer_prompt",
    )(proj_a, proj_a, proj_b, proj_b, proj_b, proj_b, proj_b, sinks, x, proj_a, proj_b, wau, wru, wout, gpost)


def _ffn_kernel(h_ref, p_ref, wg_ref, wu_ref, wo_ref, gpre_ref, gpost_ref, gple_ref, wpg_ref, wple_ref,
                y_ref, t_ref, acc_ref, *, n_f):
    j = pl.program_id(1)

    @pl.when(j == 0)
    def _():
        t_ref[...] = _rms(h_ref[...], gpre_ref[...]).astype(BF16)
        acc_ref[...] = jnp.zeros_like(acc_ref)

    tt = t_ref[...]
    g = _dot(tt, wg_ref[...])
    up = _dot(tt, wu_ref[...])
    act = (g * _sigmoid(g) * up).astype(BF16)
    acc_ref[...] += _dot(act, wo_ref[...])

    @pl.when(j == n_f - 1)
    def _():
        h2 = h_ref[...] + _rms(acc_ref[...], gpost_ref[...])
        z = _rms(h2, gple_ref[...]).astype(BF16)
        gate = _sigmoid(_dot(z, wpg_ref[...]))
        y_ref[...] = h2 + gate * _dot(p_ref[...].astype(BF16), wple_ref[...])


def _ffn(h, ple, wff_in, wff_out, gpre, gpost, gple, wpg, wple, *, tm, tf):
    m = h.shape[0]
    n_f = D_FF // tf
    assert m % tm == 0 and D_FF % tf == 0
    const = lambda shape: pl.BlockSpec(shape, lambda i, j: (0,) * len(shape), pipeline_mode=pl.Buffered(1))
    return pl.pallas_call(
        functools.partial(_ffn_kernel, n_f=n_f),
        grid=(m // tm, n_f),
        in_specs=[
            pl.BlockSpec((tm, D_MODEL), lambda i, j: (i, 0)),
            pl.BlockSpec((tm, PLE_DIM), lambda i, j: (i, 0)),
            pl.BlockSpec((D_MODEL, tf), lambda i, j: (0, j)),
            pl.BlockSpec((D_MODEL, tf), lambda i, j: (0, j + n_f)),
            pl.BlockSpec((tf, D_MODEL), lambda i, j: (j, 0)),
            const((1, D_MODEL)),
            const((1, D_MODEL)),
            const((1, D_MODEL)),
            const((D_MODEL, D_MODEL)),
            const((PLE_DIM, D_MODEL)),
        ],
        out_specs=pl.BlockSpec((tm, D_MODEL), lambda i, j: (i, 0)),
        out_shape=jax.ShapeDtypeStruct((m, D_MODEL), F32),
        scratch_shapes=[pltpu.VMEM((tm, D_MODEL), BF16), pltpu.VMEM((tm, D_MODEL), F32)],
        compiler_params=_params(),
        name="ffn_ple",
    )(h, ple, wff_in, wff_in, wff_out, gpre, gpost, gple, wpg, wple)


def _layer(x, ple, w, *, n_b, seq, cache=None):
    is_prompt = cache is None
    proj_a, proj_b, kv = _in_proj(x, w["g_pre_mix"], *w["w_in"], tm_a=512, tm_b=1024 if is_prompt else 512, n_tiles=2)
    if is_prompt:
        h, state = _mixer(x, proj_a, proj_b, w["sinks"], w["w_a_up"], w["w_r_up"], w["w_out"], w["g_post_mix"],
                          n_b=n_b, n_t=seq // 256, t_tok=256, q_blk=SWA_ROWS, r_chunk=256)
    else:
        a, r, state = _seq_mix(proj_a, proj_b, w["sinks"], cache, n_b=n_b, t_tok=seq)
        h = _merge(x, a, r, proj_a, proj_b, w["w_a_up"], w["w_r_up"], w["w_out"], w["g_post_mix"], tm=512)
    y = _ffn(h, ple, w["w_ffn_in"], w["w_ffn_out"], w["g_pre_ffn"], w["g_post_ffn"], w["g_ple"],
             w["w_ple_gate"], w["w_ple"], tm=512, tf=512)
    kv = kv.reshape(n_b, seq, KV_COLS)[:, max(seq - SWA_ROWS, 0):]
    return y, kv[:, :, :KV_A_DIM], kv[:, :, KV_A_DIM:], state


def kernel(x_prompt, x_sample, cache_swa_k, cache_swa_v, state_ret, p_prompt, p_sample, norm_pre_mix, w_in, attn_sinks, w_a_up, w_r_up, w_out, norm_post_mix, norm_pre_ffn, w_ffn_in, w_ffn_out, norm_post_ffn, norm_ple, w_ple_gate, w_ple):
    depth = w_in.shape[0]
    n_bp, seq_p, _ = x_prompt.shape
    n_bs, seq_s, _ = x_sample.shape
    hp = x_prompt.reshape(n_bp * seq_p, D_MODEL)
    hs = x_sample.reshape(n_bs * seq_s, D_MODEL)
    at = (lambda a, l: a.reshape(a.shape[1:])) if depth == 1 else (lambda a, l: a[l])
    outs = [[] for _ in range(6)]
    for l in range(depth):
        w = {
            "w_in": _regroup_w_in(at(w_in, l)),
            "w_a_up": at(w_a_up, l).astype(BF16),
            "w_r_up": at(w_r_up, l).astype(BF16),
            "w_out": at(w_out, l).astype(BF16),
            "w_ffn_in": at(w_ffn_in, l).astype(BF16),
            "w_ffn_out": at(w_ffn_out, l).astype(BF16),
            "w_ple_gate": at(w_ple_gate, l).astype(BF16),
            "w_ple": at(w_ple, l).astype(BF16),
            "sinks": at(attn_sinks, l),
            "g_pre_mix": at(norm_pre_mix, l).reshape(1, D_MODEL),
            "g_post_mix": at(norm_post_mix, l).reshape(1, D_MODEL),
            "g_pre_ffn": at(norm_pre_ffn, l).reshape(1, D_MODEL),
            "g_post_ffn": at(norm_post_ffn, l).reshape(1, D_MODEL),
            "g_ple": at(norm_ple, l).reshape(1, D_MODEL),
        }
        hp, kp, vp, sp = _layer(hp, at(p_prompt, l).reshape(n_bp * seq_p, PLE_DIM), w, n_b=n_bp, seq=seq_p)
        cache = (at(cache_swa_k, l).reshape(n_bs, SWA_ROWS, KV_A_DIM), at(cache_swa_v, l).reshape(n_bs, SWA_ROWS, KV_A_DIM),
                 at(state_ret, l))
        hs, k_s, v_s, s_s = _layer(hs, at(p_sample, l).reshape(n_bs * seq_s, PLE_DIM), w, n_b=n_bs, seq=seq_s, cache=cache)
        kv_shape = (SWA_ROWS, N_KV_HEADS, HEAD_DIM)
        outs[0].append(kp.reshape(n_bp, *kv_shape))
        outs[1].append(vp.reshape(n_bp, *kv_shape))
        outs[2].append(sp)
        outs[3].append(jnp.concatenate([cache[0][:, seq_s:], k_s], axis=1).reshape(n_bs, *kv_shape))
        outs[4].append(jnp.concatenate([cache[1][:, seq_s:], v_s], axis=1).reshape(n_bs, *kv_shape))
        outs[5].append(s_s)
    stack = (lambda o: o[0][None]) if depth == 1 else jnp.stack
    return (hp.reshape(n_bp, seq_p, D_MODEL), hs.reshape(n_bs, seq_s, D_MODEL), *(stack(o) for o in outs))
```

```python
import functools

import jax
import jax.numpy as jnp
import numpy as np
from jax import lax
from jax.experimental import pallas as pl
from jax.experimental.pallas import tpu as pltpu

F32 = jnp.float32
BF16 = jnp.bfloat16

D_MODEL = 2048
CHUNK = 64
PLE_DIM = 256
N_Q_HEADS = 16
N_KV_HEADS = 4
GROUP = N_Q_HEADS // N_KV_HEADS
HEAD_DIM = 64
SWA_ROWS = 128
Q_A_DIM = N_Q_HEADS * HEAD_DIM
KV_A_DIM = N_KV_HEADS * HEAD_DIM
N_RET_HEADS = 8
RET_DK = 128
RET_DV = 256
RET_QK_DIM = N_RET_HEADS * RET_DK
RET_V_DIM = N_RET_HEADS * RET_DV
D_FF = ((8 * D_MODEL // 3 + 255) // 256) * 256
D_IN = Q_A_DIM + 2 * KV_A_DIM + 2 * RET_QK_DIM + 2 * RET_V_DIM + 2 * D_MODEL
NEG_INF = -1e30
EPS = 1e-6

KV_COLS = 2 * KV_A_DIM
PROJ_A_COLS = 2 * RET_V_DIM + D_MODEL
PROJ_B_COLS = D_IN - PROJ_A_COLS
A_VR, A_GR, A_GA = 0, 1, 2
B_GB = 0
B_QA, B_QR, B_KR = 2, 3, 4
B_KV = (PROJ_B_COLS - KV_COLS) // KV_COLS

V7X_VMEM_LIMIT_BYTES = 60 * 1024 * 1024
V7X_VMEM_FFN_LIMIT_BYTES = 63 * 1024 * 1024

_ALIBI_SLOPES = [float(v) for v in np.exp2(np.float32(-8.0) * (np.arange(N_Q_HEADS, dtype=np.float32) + 1.0) / N_Q_HEADS)]
_LOG_GAMMA = [float(v) for v in np.log(np.float32(1.0) - np.exp2(np.float32(-5.0) - np.arange(N_RET_HEADS, dtype=np.float32)))]
_ATTN_SCALE = HEAD_DIM ** -0.5
_RET_SCALE = RET_DK ** -0.5


def _dot(a, b):
    return jnp.dot(a, b, preferred_element_type=F32)


def _dot_nt(a, b):
    return lax.dot_general(a, b, (((1,), (1,)), ((), ())), preferred_element_type=F32)


def _dot_tn(a, b):
    return lax.dot_general(a, b, (((0,), (0,)), ((), ())), preferred_element_type=F32)


def _rms(x, g):
    return x * lax.rsqrt(jnp.mean(x * x, axis=-1, keepdims=True) + EPS) * g


def _sigmoid(x):
    return 1.0 / (1.0 + jnp.exp(-x))


def _params(grid_rank=2, vmem_limit_bytes=V7X_VMEM_LIMIT_BYTES):
    return pltpu.CompilerParams(dimension_semantics=("arbitrary",) * grid_rank,
                                vmem_limit_bytes=vmem_limit_bytes)


def _regroup_kernel(w_ref, oa_ref, ob_ref, *, n_a):
    j = pl.program_id(0)

    @pl.when(j < n_a)
    def _():
        oa_ref[...] = w_ref[...].astype(BF16)

    @pl.when(j >= n_a)
    def _():
        ob_ref[...] = w_ref[...].astype(BF16)


def _regroup_w_in(w):
    n_tiles = D_IN // KV_COLS
    n_a = PROJ_A_COLS // KV_COLS
    ret_v_tile = (Q_A_DIM + KV_COLS + 2 * RET_QK_DIM) // KV_COLS
    n_wide = n_tiles - ret_v_tile
    n_q = Q_A_DIM // KV_COLS

    def src_tile(j):
        return jnp.where(j < n_wide, j + ret_v_tile,
                         jnp.where(j < n_wide + n_q, j - n_wide,
                                   jnp.where(j < n_tiles - 1, j - n_wide + 1, n_q)))

    return pl.pallas_call(
        functools.partial(_regroup_kernel, n_a=n_a),
        grid=(n_tiles,),
        in_specs=[pl.BlockSpec((D_MODEL, KV_COLS), lambda j: (0, src_tile(j)))],
        out_specs=[
            pl.BlockSpec((D_MODEL, KV_COLS), lambda j: (0, jnp.minimum(j, n_a - 1))),
            pl.BlockSpec((D_MODEL, KV_COLS), lambda j: (0, jnp.maximum(j - n_a, 0))),
        ],
        out_shape=[
            jax.ShapeDtypeStruct((D_MODEL, PROJ_A_COLS), BF16),
            jax.ShapeDtypeStruct((D_MODEL, PROJ_B_COLS), BF16),
        ],
        compiler_params=_params(1),
        name="regroup_w_in",
    )(w)


def _in_proj_a_kernel(x_ref, g_ref, w_ref, o_ref, u_ref):
    @pl.when(pl.program_id(1) == 0)
    def _():
        u_ref[...] = _rms(x_ref[...], g_ref[...]).astype(BF16)

    o_ref[...] = _dot(u_ref[...], w_ref[...]).astype(BF16)


def _in_proj_b_kernel(u_ref, w_ref, o_ref, kv_ref, *, kv_tile, kv_off):
    acc = _dot(u_ref[...], w_ref[...])
    o_ref[...] = acc.astype(BF16)

    @pl.when(pl.program_id(1) == kv_tile)
    def _():
        kv_ref[...] = acc[:, kv_off:kv_off + KV_COLS]


def _in_proj(x, g, w_a, w_b, *, tm_a, tm_b, n_tiles_a, n_tiles_b):
    m = x.shape[0]
    tn_a, tn_b = PROJ_A_COLS // n_tiles_a, PROJ_B_COLS // n_tiles_b
    kv_start = PROJ_B_COLS - KV_COLS
    kv_tile, kv_off = kv_start // tn_b, kv_start % tn_b
    assert m % tm_a == 0 and m % tm_b == 0 and PROJ_A_COLS % n_tiles_a == 0 and PROJ_B_COLS % n_tiles_b == 0
    assert kv_off + KV_COLS <= tn_b
    proj_a, u = pl.pallas_call(
        _in_proj_a_kernel,
        grid=(m // tm_a, n_tiles_a),
        in_specs=[
            pl.BlockSpec((tm_a, D_MODEL), lambda i, j: (i, 0)),
            pl.BlockSpec((1, D_MODEL), lambda i, j: (0, 0)),
            pl.BlockSpec((D_MODEL, tn_a), lambda i, j: (0, j)),
        ],
        out_specs=[
            pl.BlockSpec((tm_a, tn_a), lambda i, j: (i, j)),
            pl.BlockSpec((tm_a, D_MODEL), lambda i, j: (i, 0)),
        ],
        out_shape=[
            jax.ShapeDtypeStruct((m, PROJ_A_COLS), BF16),
            jax.ShapeDtypeStruct((m, D_MODEL), BF16),
        ],
        compiler_params=_params(),
        name="in_proj_a",
    )(x, g, w_a)
    proj_b, kv = pl.pallas_call(
        functools.partial(_in_proj_b_kernel, kv_tile=kv_tile, kv_off=kv_off),
        grid=(m // tm_b, n_tiles_b),
        in_specs=[
            pl.BlockSpec((tm_b, D_MODEL), lambda i, j: (i, 0)),
            pl.BlockSpec((D_MODEL, tn_b), lambda i, j: (0, j)),
        ],
        out_specs=[
            pl.BlockSpec((tm_b, tn_b), lambda i, j: (i, j)),
            pl.BlockSpec((tm_b, KV_COLS), lambda i, j: (i, 0)),
        ],
        out_shape=[
            jax.ShapeDtypeStruct((m, PROJ_B_COLS), BF16),
            jax.ShapeDtypeStruct((m, KV_COLS), F32),
        ],
        compiler_params=_params(),
        name="in_proj_b",
    )(u, w_b)
    return proj_a, proj_b, kv


def _fill_tables(bias_ref, pen_ref, dmat_ref, cdec_ref, kdec_ref, *, q_blk, r_chunk):
    s_keys = SWA_ROWS + q_blk
    qi = lax.broadcasted_iota(jnp.int32, (s_keys, q_blk), 1) + SWA_ROWS
    ki = lax.broadcasted_iota(jnp.int32, (s_keys, q_blk), 0)
    dist = jnp.abs(qi - ki).astype(F32)
    qc = qi // CHUNK
    kc = ki // CHUNK
    band = (kc >= qc - SWA_ROWS // CHUNK) & (kc <= qc)
    for hd in range(N_Q_HEADS):
        bias_ref[hd] = jnp.where(band, -_ALIBI_SLOPES[hd] * dist, NEG_INF)
    if pen_ref is not None:
        pen_ref[0] = jnp.zeros((s_keys, q_blk), F32)
        pen_ref[1] = jnp.where(ki >= SWA_ROWS, 0.0, NEG_INF)

    diff = (lax.broadcasted_iota(jnp.int32, (r_chunk, r_chunk), 0)
            - lax.broadcasted_iota(jnp.int32, (r_chunk, r_chunk), 1)).astype(F32)
    row_k = lax.broadcasted_iota(jnp.int32, (r_chunk, RET_DK), 0).astype(F32)
    for h in range(N_RET_HEADS):
        lg = _LOG_GAMMA[h]
        dmat_ref[h] = jnp.where(diff >= 0.0, jnp.exp(lg * jnp.maximum(diff, 0.0)), 0.0) * _RET_SCALE
        cdec_ref[h] = jnp.exp(lg * (row_k + 1.0))
        kdec_ref[h] = jnp.exp(lg * (r_chunk - 1.0 - row_k)) * _RET_SCALE


def _seq_mix_items(*, vr_ref, gr_ref, qa_ref, qr_ref, kr_ref, kf_ref, sink_ref, a_ref, r_ref, state_ref,
                   bias_ref, pen_ref, dmat_ref, cdec_ref, kdec_ref, first_tile, t_tok, q_blk, r_chunk):
    s_keys = SWA_ROWS + q_blk

    def attention_group(qb, g):
        r0 = qb * q_blk
        k_g = kf_ref[r0:r0 + s_keys, g * HEAD_DIM:(g + 1) * HEAD_DIM]
        v_g = kf_ref[r0:r0 + s_keys, KV_A_DIM + g * HEAD_DIM:KV_A_DIM + (g + 1) * HEAD_DIM]
        q_g = qa_ref[r0:r0 + q_blk, g * GROUP * HEAD_DIM:(g + 1) * GROUP * HEAD_DIM] * jnp.asarray(_ATTN_SCALE, BF16)
        vt_g = v_g.T
        for jh in range(GROUP):
            hd = g * GROUP + jh
            q_h = q_g[:, jh * HEAD_DIM:(jh + 1) * HEAD_DIM]
            s = _dot_nt(k_g, q_h) + bias_ref[hd]
            if pen_ref is not None and qb == 0:
                s = s + pen_ref[first_tile]
            sink = sink_ref[hd]
            m = jnp.maximum(jnp.max(s, axis=0, keepdims=True), sink)
            e = jnp.exp(s - m)
            den = jnp.sum(e, axis=0, keepdims=True) + jnp.exp(sink - m)
            o_t = _dot(vt_g, e.astype(BF16)) * (1.0 / den)
            a_ref[r0:r0 + q_blk, hd * HEAD_DIM:(hd + 1) * HEAD_DIM] = o_t.T.astype(BF16)

    def retention_head(c, h):
        lg = _LOG_GAMMA[h]
        rows = slice(c * r_chunk, (c + 1) * r_chunk)
        q = qr_ref[rows, h * RET_DK:(h + 1) * RET_DK]
        k = kr_ref[rows, h * RET_DK:(h + 1) * RET_DK]
        v = vr_ref[rows, h * RET_DV:(h + 1) * RET_DV]
        st = state_ref[h]
        sc = _dot_nt(q, k) * dmat_ref[h]
        cdec = cdec_ref[h]
        o = _dot(sc.astype(BF16), v) + _dot(q, st.astype(BF16)) * jnp.concatenate([cdec] * (RET_DV // RET_DK), axis=1)
        kd = (k.astype(F32) * kdec_ref[h]).astype(BF16)
        state_ref[h] = st * float(np.exp(np.float32(lg) * np.float32(r_chunk))) + _dot_tn(kd, v)
        c_o = o - jnp.mean(o, axis=-1, keepdims=True)
        nrm = c_o * lax.rsqrt(jnp.mean(c_o * c_o, axis=-1, keepdims=True) + EPS)
        gr = gr_ref[rows, h * RET_DV:(h + 1) * RET_DV].astype(F32)
        r_ref[rows, h * RET_DV:(h + 1) * RET_DV] = (nrm * (gr * _sigmoid(gr))).astype(BF16)

    attn_groups = [(qb, g) for qb in range(t_tok // q_blk) for g in range(N_KV_HEADS)]
    ret_heads = [(c, h) for c in range(t_tok // r_chunk) for h in range(N_RET_HEADS)]
    items = []
    for i in range(max(len(attn_groups), len(ret_heads))):
        if i < len(ret_heads):
            items.append(functools.partial(retention_head, *ret_heads[i]))
        if i < len(attn_groups):
            items.append(functools.partial(attention_group, *attn_groups[i]))
    return items


def _seq_scratch(t_tok, q_blk, r_chunk, with_pen):
    shapes = [
        pltpu.VMEM((N_RET_HEADS, RET_DK, RET_DV), F32),
        pltpu.VMEM((SWA_ROWS + t_tok, KV_COLS), BF16),
        pltpu.VMEM((N_Q_HEADS, SWA_ROWS + q_blk, q_blk), F32),
        pltpu.VMEM((N_RET_HEADS, r_chunk, r_chunk), F32),
        pltpu.VMEM((N_RET_HEADS, r_chunk, RET_DK), F32),
        pltpu.VMEM((N_RET_HEADS, r_chunk, RET_DK), F32),
    ]
    if with_pen:
        shapes.append(pltpu.VMEM((2, SWA_ROWS + q_blk, q_blk), F32))
    return shapes


def _seq_mix_kernel(vr_ref, gr_ref, qa_ref, qr_ref, kr_ref, kv_ref, ck_ref, cv_ref, st_in_ref, sink_ref,
                    a_ref, r_ref, st_out_ref,
                    state_ref, kf_ref, bias_ref, dmat_ref, cdec_ref, kdec_ref, *, t_tok):
    @pl.when(pl.program_id(0) == 0)
    def _():
        _fill_tables(bias_ref, None, dmat_ref, cdec_ref, kdec_ref, q_blk=t_tok, r_chunk=t_tok)

    state_ref[...] = st_in_ref[...]
    kf_ref[0:SWA_ROWS, 0:KV_A_DIM] = ck_ref[...].astype(BF16)
    kf_ref[0:SWA_ROWS, KV_A_DIM:KV_COLS] = cv_ref[...].astype(BF16)
    kf_ref[SWA_ROWS:SWA_ROWS + t_tok, :] = kv_ref[...]
    for item in _seq_mix_items(
            vr_ref=vr_ref, gr_ref=gr_ref, qa_ref=qa_ref, qr_ref=qr_ref, kr_ref=kr_ref, kf_ref=kf_ref,
            sink_ref=sink_ref, a_ref=a_ref, r_ref=r_ref, state_ref=state_ref, bias_ref=bias_ref, pen_ref=None,
            dmat_ref=dmat_ref, cdec_ref=cdec_ref, kdec_ref=kdec_ref, first_tile=None,
            t_tok=t_tok, q_blk=t_tok, r_chunk=t_tok):
        item()
    st_out_ref[...] = state_ref[...]


def _seq_mix(proj_a, proj_b, sinks, cache, *, n_b, t_tok):
    ck, cv, st_in = cache
    m = n_b * t_tok
    return pl.pallas_call(
        functools.partial(_seq_mix_kernel, t_tok=t_tok),
        grid=(n_b,),
        in_specs=[
            pl.BlockSpec((t_tok, RET_V_DIM), lambda b: (b, A_VR)),
            pl.BlockSpec((t_tok, RET_V_DIM), lambda b: (b, A_GR)),
            pl.BlockSpec((t_tok, Q_A_DIM), lambda b: (b, B_QA)),
            pl.BlockSpec((t_tok, RET_QK_DIM), lambda b: (b, B_QR)),
            pl.BlockSpec((t_tok, RET_QK_DIM), lambda b: (b, B_KR)),
            pl.BlockSpec((t_tok, KV_COLS), lambda b: (b, B_KV)),
            pl.BlockSpec((None, SWA_ROWS, KV_A_DIM), lambda b: (b, 0, 0)),
            pl.BlockSpec((None, SWA_ROWS, KV_A_DIM), lambda b: (b, 0, 0)),
            pl.BlockSpec((None, N_RET_HEADS, RET_DK, RET_DV), lambda b: (b, 0, 0, 0)),
            pl.BlockSpec(memory_space=pltpu.SMEM),
        ],
        out_specs=[
            pl.BlockSpec((t_tok, Q_A_DIM), lambda b: (b, 0)),
            pl.BlockSpec((t_tok, RET_V_DIM), lambda b: (b, 0)),
            pl.BlockSpec((None, N_RET_HEADS, RET_DK, RET_DV), lambda b: (b, 0, 0, 0)),
        ],
        out_shape=[
            jax.ShapeDtypeStruct((m, Q_A_DIM), BF16),
            jax.ShapeDtypeStruct((m, RET_V_DIM), BF16),
            jax.ShapeDtypeStruct((n_b, N_RET_HEADS, RET_DK, RET_DV), F32),
        ],
        scratch_shapes=_seq_scratch(t_tok, t_tok, t_tok, with_pen=False),
        compiler_params=_params(1),
        name="seq_mix_sample",
    )(proj_a, proj_a, proj_b, proj_b, proj_b, proj_b, ck, cv, st_in, sinks)


def _merge_math(x, a, r, ga, gb, wau_ref, wru_ref, wout_ref, gpost_ref):
    merged = (_sigmoid(ga.astype(F32)) * _dot(a, wau_ref[...])
              + _sigmoid(gb.astype(F32)) * _dot(r, wru_ref[...]))
    mo = _dot(merged.astype(BF16), wout_ref[...])
    return x + _rms(mo, gpost_ref[...])


def _merge_kernel(x_ref, a_ref, r_ref, ga_ref, gb_ref, wau_ref, wru_ref, wout_ref, gpost_ref, h_ref):
    h_ref[...] = _merge_math(x_ref[...], a_ref[...], r_ref[...], ga_ref[...], gb_ref[...],
                             wau_ref, wru_ref, wout_ref, gpost_ref)


def _merge(x, a, r, proj_a, proj_b, wau, wru, wout, gpost, *, tm):
    m = x.shape[0]
    assert m % tm == 0
    const = lambda shape: pl.BlockSpec(shape, lambda i: (0,) * len(shape), pipeline_mode=pl.Buffered(1))
    return pl.pallas_call(
        _merge_kernel,
        grid=(m // tm,),
        in_specs=[
            pl.BlockSpec((tm, D_MODEL), lambda i: (i, 0)),
            pl.BlockSpec((tm, Q_A_DIM), lambda i: (i, 0)),
            pl.BlockSpec((tm, RET_V_DIM), lambda i: (i, 0)),
            pl.BlockSpec((tm, D_MODEL), lambda i: (i, A_GA)),
            pl.BlockSpec((tm, D_MODEL), lambda i: (i, B_GB)),
            const((Q_A_DIM, D_MODEL)),
            const((RET_V_DIM, D_MODEL)),
            const((D_MODEL, D_MODEL)),
            const((1, D_MODEL)),
        ],
        out_specs=pl.BlockSpec((tm, D_MODEL), lambda i: (i, 0)),
        out_shape=jax.ShapeDtypeStruct((m, D_MODEL), F32),
        compiler_params=_params(1),
        name="merge",
    )(x, a, r, proj_a, proj_b, wau, wru, wout, gpost)


def _mixer_kernel(vr_ref, gr_ref, qa_ref, qr_ref, kr_ref, kv_ref, kvh_ref, sink_ref,
                  x_ref, ga_ref, gb_ref, wau_ref, wru_ref, wout_ref, gpost_ref,
                  h_ref, st_out_ref,
                  state_ref, kf_ref, bias_ref, dmat_ref, cdec_ref, kdec_ref, pen_ref, a_scr, r_scr,
                  *, t_tok, q_blk, r_chunk, n_t, n_tiles):
    s = pl.program_id(0)
    t = lax.rem(jnp.minimum(s, n_tiles - 1), n_t)

    @pl.when(s == 0)
    def _():
        _fill_tables(bias_ref, pen_ref, dmat_ref, cdec_ref, kdec_ref, q_blk=q_blk, r_chunk=r_chunk)
        a_scr[1] = jnp.zeros(a_scr.shape[1:], BF16)
        r_scr[1] = jnp.zeros(r_scr.shape[1:], BF16)

    @pl.when(t == 0)
    def _():
        state_ref[...] = jnp.zeros_like(state_ref)

    kf_ref[0:SWA_ROWS, :] = kvh_ref[...]
    kf_ref[SWA_ROWS:SWA_ROWS + t_tok, :] = kv_ref[...]
    first_tile = jnp.where(t == 0, 1, 0)

    def step(w, r):
        h_ref[...] = _merge_math(x_ref[...], a_scr[r], r_scr[r], ga_ref[...], gb_ref[...],
                                 wau_ref, wru_ref, wout_ref, gpost_ref)
        for item in _seq_mix_items(
                vr_ref=vr_ref, gr_ref=gr_ref, qa_ref=qa_ref, qr_ref=qr_ref, kr_ref=kr_ref, kf_ref=kf_ref,
                sink_ref=sink_ref, a_ref=a_scr.at[w], r_ref=r_scr.at[w], state_ref=state_ref, bias_ref=bias_ref,
                pen_ref=pen_ref, dmat_ref=dmat_ref, cdec_ref=cdec_ref, kdec_ref=kdec_ref, first_tile=first_tile,
                t_tok=t_tok, q_blk=q_blk, r_chunk=r_chunk):
            item()

    parity = lax.rem(s, 2)
    pl.when(parity == 0)(functools.partial(step, 0, 1))
    pl.when(parity == 1)(functools.partial(step, 1, 0))

    @pl.when((t == n_t - 1) & (s < n_tiles))
    def _():
        st_out_ref[...] = state_ref[...]


def _mixer(x, proj_a, proj_b, sinks, wau, wru, wout, gpost, *, n_b, n_t, t_tok, q_blk, r_chunk):
    assert t_tok % q_blk == 0 and t_tok % r_chunk == 0 and q_blk == SWA_ROWS
    n_tiles = n_b * n_t
    halo_per_tile = t_tok // SWA_ROWS
    seq_tile = lambda s: jnp.minimum(s, n_tiles - 1)
    mrg_tile = lambda s: jnp.maximum(s - 1, 0)
    const = lambda shape: pl.BlockSpec(shape, lambda s: (0,) * len(shape), pipeline_mode=pl.Buffered(1))
    return pl.pallas_call(
        functools.partial(_mixer_kernel, t_tok=t_tok, q_blk=q_blk, r_chunk=r_chunk, n_t=n_t, n_tiles=n_tiles),
        grid=(n_tiles + 1,),
        in_specs=[
            pl.BlockSpec((t_tok, RET_V_DIM), lambda s: (seq_tile(s), A_VR)),
            pl.BlockSpec((t_tok, RET_V_DIM), lambda s: (seq_tile(s), A_GR)),
            pl.BlockSpec((t_tok, Q_A_DIM), lambda s: (seq_tile(s), B_QA)),
            pl.BlockSpec((t_tok, RET_QK_DIM), lambda s: (seq_tile(s), B_QR)),
            pl.BlockSpec((t_tok, RET_QK_DIM), lambda s: (seq_tile(s), B_KR)),
            pl.BlockSpec((t_tok, KV_COLS), lambda s: (seq_tile(s), B_KV)),
            pl.BlockSpec((SWA_ROWS, KV_COLS), lambda s: (jnp.maximum(seq_tile(s) * halo_per_tile - 1, 0), B_KV)),
            pl.BlockSpec(memory_space=pltpu.SMEM),
            pl.BlockSpec((t_tok, D_MODEL), lambda s: (mrg_tile(s), 0)),
            pl.BlockSpec((t_tok, D_MODEL), lambda s: (mrg_tile(s), A_GA)),
            pl.BlockSpec((t_tok, D_MODEL), lambda s: (mrg_tile(s), B_GB)),
            const((Q_A_DIM, D_MODEL)),
            const((RET_V_DIM, D_MODEL)),
            const((D_MODEL, D_MODEL)),
            const((1, D_MODEL)),
        ],
        out_specs=[
            pl.BlockSpec((t_tok, D_MODEL), lambda s: (mrg_tile(s), 0)),
            pl.BlockSpec((None, N_RET_HEADS, RET_DK, RET_DV), lambda s: (seq_tile(s) // n_t, 0, 0, 0)),
        ],
        out_shape=[
            jax.ShapeDtypeStruct((n_tiles * t_tok, D_MODEL), F32),
            jax.ShapeDtypeStruct((n_b, N_RET_HEADS, RET_DK, RET_DV), F32),
        ],
        scratch_shapes=_seq_scratch(t_tok, q_blk, r_chunk, with_pen=True) + [
            pltpu.VMEM((2, t_tok, Q_A_DIM), BF16),
            pltpu.VMEM((2, t_tok, RET_V_DIM), BF16),
        ],
        compiler_params=_params(1),
        name="mixer_prompt",
    )(proj_a, proj_a, proj_b, proj_b, proj_b, proj_b, proj_b, sinks, x, proj_a, proj_b, wau, wru, wout, gpost)


def _ffn_kernel(h_ref, wg_ref, wu_ref, wo_ref, gpre_ref, gpost_ref, y_ref, t_ref, *, n_f):
    j = pl.program_id(1)

    @pl.when(j == 0)
    def _():
        t_ref[...] = _rms(h_ref[...], gpre_ref[...]).astype(BF16)
        y_ref[...] = jnp.zeros_like(y_ref)

    tt = t_ref[...]
    g = _dot(tt, wg_ref[...])
    up = _dot(tt, wu_ref[...])
    act = (g * _sigmoid(g) * up).astype(BF16)
    y_ref[...] += _dot(act, wo_ref[...])

    @pl.when(j == n_f - 1)
    def _():
        y_ref[...] = h_ref[...] + _rms(y_ref[...], gpost_ref[...])


def _ffn(h, wff_in, wff_out, gpre, gpost, *, tm, tf):
    m = h.shape[0]
    n_f = D_FF // tf
    assert m % tm == 0 and D_FF % tf == 0
    const = lambda shape: pl.BlockSpec(shape, lambda i, j: (0,) * len(shape), pipeline_mode=pl.Buffered(1))
    return pl.pallas_call(
        functools.partial(_ffn_kernel, n_f=n_f),
        grid=(m // tm, n_f),
        in_specs=[
            pl.BlockSpec((tm, D_MODEL), lambda i, j: (i, 0)),
            pl.BlockSpec((D_MODEL, tf), lambda i, j: (0, j)),
            pl.BlockSpec((D_MODEL, tf), lambda i, j: (0, j + n_f)),
            pl.BlockSpec((tf, D_MODEL), lambda i, j: (j, 0)),
            const((1, D_MODEL)),
            const((1, D_MODEL)),
        ],
        out_specs=pl.BlockSpec((tm, D_MODEL), lambda i, j: (i, 0)),
        out_shape=jax.ShapeDtypeStruct((m, D_MODEL), F32),
        scratch_shapes=[pltpu.VMEM((tm, D_MODEL), BF16)],
        compiler_params=_params(vmem_limit_bytes=V7X_VMEM_FFN_LIMIT_BYTES),
        name="ffn",
    )(h, wff_in, wff_in, wff_out, gpre, gpost)


def _ple_kernel(h_ref, p_ref, gple_ref, wpg_ref, wple_ref, y_ref):
    h = h_ref[...]
    gate = _sigmoid(_dot(_rms(h, gple_ref[...]).astype(BF16), wpg_ref[...]))
    y_ref[...] = h + gate * _dot(p_ref[...].astype(BF16), wple_ref[...])


def _ple(h, ple, gple, wpg, wple, *, tm):
    m = h.shape[0]
    assert m % tm == 0
    const = lambda shape: pl.BlockSpec(shape, lambda i: (0,) * len(shape), pipeline_mode=pl.Buffered(1))
    return pl.pallas_call(
        _ple_kernel,
        grid=(m // tm,),
        in_specs=[
            pl.BlockSpec((tm, D_MODEL), lambda i: (i, 0)),
            pl.BlockSpec((tm, PLE_DIM), lambda i: (i, 0)),
            const((1, D_MODEL)),
            const((D_MODEL, D_MODEL)),
            const((PLE_DIM, D_MODEL)),
        ],
        out_specs=pl.BlockSpec((tm, D_MODEL), lambda i: (i, 0)),
        out_shape=jax.ShapeDtypeStruct((m, D_MODEL), F32),
        compiler_params=_params(1),
        name="ple",
    )(h, ple, gple, wpg, wple)


def _layer(x, ple, w, *, n_b, seq, cache=None):
    is_prompt = cache is None
    tm = 1024 if x.shape[0] % 1024 == 0 else 512
    proj_a, proj_b, kv = _in_proj(x, w["g_pre_mix"], *w["w_in"], tm_a=tm, tm_b=tm, n_tiles_a=3, n_tiles_b=2)
    if is_prompt:
        h, state = _mixer(x, proj_a, proj_b, w["sinks"], w["w_a_up"], w["w_r_up"], w["w_out"], w["g_post_mix"],
                          n_b=n_b, n_t=seq // 256, t_tok=256, q_blk=SWA_ROWS, r_chunk=256)
    else:
        a, r, state = _seq_mix(proj_a, proj_b, w["sinks"], cache, n_b=n_b, t_tok=seq)
        h = _merge(x, a, r, proj_a, proj_b, w["w_a_up"], w["w_r_up"], w["w_out"], w["g_post_mix"], tm=512)
    h = _ffn(h, w["w_ffn_in"], w["w_ffn_out"], w["g_pre_ffn"], w["g_post_ffn"], tm=tm, tf=512)
    y = _ple(h, ple, w["g_ple"], w["w_ple_gate"], w["w_ple"], tm=512)
    kv = kv.reshape(n_b, seq, KV_COLS)[:, max(seq - SWA_ROWS, 0):]
    return y, kv[:, :, :KV_A_DIM], kv[:, :, KV_A_DIM:], state


def kernel(x_prompt, x_sample, cache_swa_k, cache_swa_v, state_ret, p_prompt, p_sample, norm_pre_mix, w_in, attn_sinks, w_a_up, w_r_up, w_out, norm_post_mix, norm_pre_ffn, w_ffn_in, w_ffn_out, norm_post_ffn, norm_ple, w_ple_gate, w_ple):
    depth = w_in.shape[0]
    n_bp, seq_p, _ = x_prompt.shape
    n_bs, seq_s, _ = x_sample.shape
    hp = x_prompt.reshape(n_bp * seq_p, D_MODEL)
    hs = x_sample.reshape(n_bs * seq_s, D_MODEL)
    at = (lambda a, l: a.reshape(a.shape[1:])) if depth == 1 else (lambda a, l: a[l])
    outs = [[] for _ in range(6)]
    for l in range(depth):
        w = {
            "w_in": _regroup_w_in(at(w_in, l)),
            "w_a_up": at(w_a_up, l).astype(BF16),
            "w_r_up": at(w_r_up, l).astype(BF16),
            "w_out": at(w_out, l).astype(BF16),
            "w_ffn_in": at(w_ffn_in, l).astype(BF16),
            "w_ffn_out": at(w_ffn_out, l).astype(BF16),
            "w_ple_gate": at(w_ple_gate, l).astype(BF16),
            "w_ple": at(w_ple, l).astype(BF16),
            "sinks": at(attn_sinks, l),
            "g_pre_mix": at(norm_pre_mix, l).reshape(1, D_MODEL),
            "g_post_mix": at(norm_post_mix, l).reshape(1, D_MODEL),
            "g_pre_ffn": at(norm_pre_ffn, l).reshape(1, D_MODEL),
            "g_post_ffn": at(norm_post_ffn, l).reshape(1, D_MODEL),
            "g_ple": at(norm_ple, l).reshape(1, D_MODEL),
        }
        hp, kp, vp, sp = _layer(hp, at(p_prompt, l).reshape(n_bp * seq_p, PLE_DIM), w, n_b=n_bp, seq=seq_p)
        cache = (at(cache_swa_k, l).reshape(n_bs, SWA_ROWS, KV_A_DIM), at(cache_swa_v, l).reshape(n_bs, SWA_ROWS, KV_A_DIM),
                 at(state_ret, l))
        hs, k_s, v_s, s_s = _layer(hs, at(p_sample, l).reshape(n_bs * seq_s, PLE_DIM), w, n_b=n_bs, seq=seq_s, cache=cache)
        kv_shape = (SWA_ROWS, N_KV_HEADS, HEAD_DIM)
        outs[0].append(kp.reshape(n_bp, *kv_shape))
        outs[1].append(vp.reshape(n_bp, *kv_shape))
        outs[2].append(sp)
        outs[3].append(jnp.concatenate([cache[0][:, seq_s:], k_s], axis=1).reshape(n_bs, *kv_shape))
        outs[4].append(jnp.concatenate([cache[1][:, seq_s:], v_s], axis=1).reshape(n_bs, *kv_shape))
        outs[5].append(s_s)
    stack = (lambda o: o[0][None]) if depth == 1 else jnp.stack
    return (hp.reshape(n_bp, seq_p, D_MODEL), hs.reshape(n_bs, seq_s, D_MODEL), *(stack(o) for o in outs))
```

```python
import functools

import jax
import jax.numpy as jnp
import numpy as np
from jax import lax
from jax.experimental import pallas as pl
from jax.experimental.pallas import tpu as pltpu

F32 = jnp.float32
BF16 = jnp.bfloat16

D_MODEL = 2048
CHUNK = 64
PLE_DIM = 256
N_Q_HEADS = 16
N_KV_HEADS = 4
GROUP = N_Q_HEADS // N_KV_HEADS
HEAD_DIM = 64
SWA_ROWS = 128
Q_A_DIM = N_Q_HEADS * HEAD_DIM
KV_A_DIM = N_KV_HEADS * HEAD_DIM
N_RET_HEADS = 8
RET_DK = 128
RET_DV = 256
RET_QK_DIM = N_RET_HEADS * RET_DK
RET_V_DIM = N_RET_HEADS * RET_DV
D_FF = ((8 * D_MODEL // 3 + 255) // 256) * 256
D_IN = Q_A_DIM + 2 * KV_A_DIM + 2 * RET_QK_DIM + 2 * RET_V_DIM + 2 * D_MODEL
NEG_INF = -1e30
EPS = 1e-6

KV_COLS = 2 * KV_A_DIM
PROJ_A_COLS = 2 * RET_V_DIM + D_MODEL
PROJ_B_COLS = D_IN - PROJ_A_COLS
A_VR, A_GR, A_GA = 0, 1, 2
B_GB = 0
B_QA, B_QR, B_KR = 2, 3, 4
B_KV = (PROJ_B_COLS - KV_COLS) // KV_COLS

V7X_VMEM_LIMIT_BYTES = 60 * 1024 * 1024
V7X_VMEM_FFN_LIMIT_BYTES = 63 * 1024 * 1024
EDGE_ROWS = 256

_ALIBI_SLOPES = [float(v) for v in np.exp2(np.float32(-8.0) * (np.arange(N_Q_HEADS, dtype=np.float32) + 1.0) / N_Q_HEADS)]
_LOG_GAMMA = [float(v) for v in np.log(np.float32(1.0) - np.exp2(np.float32(-5.0) - np.arange(N_RET_HEADS, dtype=np.float32)))]
_ATTN_SCALE = HEAD_DIM ** -0.5
_RET_SCALE = RET_DK ** -0.5


def _dot(a, b):
    return jnp.dot(a, b, preferred_element_type=F32)


def _dot_nt(a, b):
    return lax.dot_general(a, b, (((1,), (1,)), ((), ())), preferred_element_type=F32)


def _dot_tn(a, b):
    return lax.dot_general(a, b, (((0,), (0,)), ((), ())), preferred_element_type=F32)


def _rms(x, g):
    return x * lax.rsqrt(jnp.mean(x * x, axis=-1, keepdims=True) + EPS) * g


def _sigmoid(x):
    return 1.0 / (1.0 + jnp.exp(-x))


def _params(grid_rank=2, vmem_limit_bytes=V7X_VMEM_LIMIT_BYTES):
    return pltpu.CompilerParams(dimension_semantics=("arbitrary",) * grid_rank,
                                vmem_limit_bytes=vmem_limit_bytes)


def _regroup_kernel(w_ref, oa_ref, ob_ref, *, n_a):
    j = pl.program_id(0)

    @pl.when(j < n_a)
    def _():
        oa_ref[...] = w_ref[...].astype(BF16)

    @pl.when(j >= n_a)
    def _():
        ob_ref[...] = w_ref[...].astype(BF16)


def _regroup_w_in(w):
    n_tiles = D_IN // KV_COLS
    n_a = PROJ_A_COLS // KV_COLS
    ret_v_tile = (Q_A_DIM + KV_COLS + 2 * RET_QK_DIM) // KV_COLS
    n_wide = n_tiles - ret_v_tile
    n_q = Q_A_DIM // KV_COLS

    def src_tile(j):
        return jnp.where(j < n_wide, j + ret_v_tile,
                         jnp.where(j < n_wide + n_q, j - n_wide,
                                   jnp.where(j < n_tiles - 1, j - n_wide + 1, n_q)))

    return pl.pallas_call(
        functools.partial(_regroup_kernel, n_a=n_a),
        grid=(n_tiles,),
        in_specs=[pl.BlockSpec((D_MODEL, KV_COLS), lambda j: (0, src_tile(j)))],
        out_specs=[
            pl.BlockSpec((D_MODEL, KV_COLS), lambda j: (0, jnp.minimum(j, n_a - 1))),
            pl.BlockSpec((D_MODEL, KV_COLS), lambda j: (0, jnp.maximum(j - n_a, 0))),
        ],
        out_shape=[
            jax.ShapeDtypeStruct((D_MODEL, PROJ_A_COLS), BF16),
            jax.ShapeDtypeStruct((D_MODEL, PROJ_B_COLS), BF16),
        ],
        compiler_params=_params(1),
        name="regroup_w_in",
    )(w)


def _in_proj_a_kernel(x_ref, g_ref, w_ref, o_ref, u_ref):
    j = pl.program_id(1)

    @pl.when(j == 0)
    def _():
        for c in range(u_ref.shape[0] // EDGE_ROWS):
            rows = slice(c * EDGE_ROWS, (c + 1) * EDGE_ROWS)
            u_ref[rows, :] = _rms(x_ref[rows, :], g_ref[...]).astype(BF16)
            o_ref[rows, :] = _dot(u_ref[rows, :], w_ref[...]).astype(BF16)

    @pl.when(j > 0)
    def _():
        o_ref[...] = _dot(u_ref[...], w_ref[...]).astype(BF16)


def _in_proj_b_kernel(u_ref, w_ref, o_ref, kv_ref, *, kv_tile, kv_off):
    acc = _dot(u_ref[...], w_ref[...])
    o_ref[...] = acc.astype(BF16)

    @pl.when(pl.program_id(1) == kv_tile)
    def _():
        kv_ref[...] = acc[:, kv_off:kv_off + KV_COLS]


def _in_proj(x, g, w_a, w_b, *, tm_a, tm_b, n_tiles_a, n_tiles_b):
    m = x.shape[0]
    tn_a, tn_b = PROJ_A_COLS // n_tiles_a, PROJ_B_COLS // n_tiles_b
    kv_start = PROJ_B_COLS - KV_COLS
    kv_tile, kv_off = kv_start // tn_b, kv_start % tn_b
    assert m % tm_a == 0 and m % tm_b == 0 and PROJ_A_COLS % n_tiles_a == 0 and PROJ_B_COLS % n_tiles_b == 0
    assert kv_off + KV_COLS <= tn_b
    proj_a, u = pl.pallas_call(
        _in_proj_a_kernel,
        grid=(m // tm_a, n_tiles_a),
        in_specs=[
            pl.BlockSpec((tm_a, D_MODEL), lambda i, j: (i, 0)),
            pl.BlockSpec((1, D_MODEL), lambda i, j: (0, 0)),
            pl.BlockSpec((D_MODEL, tn_a), lambda i, j: (0, j)),
        ],
        out_specs=[
            pl.BlockSpec((tm_a, tn_a), lambda i, j: (i, j)),
            pl.BlockSpec((tm_a, D_MODEL), lambda i, j: (i, 0)),
        ],
        out_shape=[
            jax.ShapeDtypeStruct((m, PROJ_A_COLS), BF16),
            jax.ShapeDtypeStruct((m, D_MODEL), BF16),
        ],
        compiler_params=_params(),
        name="in_proj_a",
    )(x, g, w_a)
    proj_b, kv = pl.pallas_call(
        functools.partial(_in_proj_b_kernel, kv_tile=kv_tile, kv_off=kv_off),
        grid=(m // tm_b, n_tiles_b),
        in_specs=[
            pl.BlockSpec((tm_b, D_MODEL), lambda i, j: (i, 0)),
            pl.BlockSpec((D_MODEL, tn_b), lambda i, j: (0, j)),
        ],
        out_specs=[
            pl.BlockSpec((tm_b, tn_b), lambda i, j: (i, j)),
            pl.BlockSpec((tm_b, KV_COLS), lambda i, j: (i, 0)),
        ],
        out_shape=[
            jax.ShapeDtypeStruct((m, PROJ_B_COLS), BF16),
            jax.ShapeDtypeStruct((m, KV_COLS), F32),
        ],
        compiler_params=_params(),
        name="in_proj_b",
    )(u, w_b)
    return proj_a, proj_b, kv


def _fill_tables(bias_ref, pen_ref, dmat_ref, cdec_ref, kdec_ref, *, q_blk, r_chunk):
    s_keys = SWA_ROWS + q_blk
    qi = lax.broadcasted_iota(jnp.int32, (s_keys, q_blk), 1) + SWA_ROWS
    ki = lax.broadcasted_iota(jnp.int32, (s_keys, q_blk), 0)
    dist = jnp.abs(qi - ki).astype(F32)
    qc = qi // CHUNK
    kc = ki // CHUNK
    band = (kc >= qc - SWA_ROWS // CHUNK) & (kc <= qc)
    for hd in range(N_Q_HEADS):
        bias_ref[hd] = jnp.where(band, -_ALIBI_SLOPES[hd] * dist, NEG_INF)
    if pen_ref is not None:
        pen_ref[0] = jnp.zeros((s_keys, q_blk), F32)
        pen_ref[1] = jnp.where(ki >= SWA_ROWS, 0.0, NEG_INF)

    diff = (lax.broadcasted_iota(jnp.int32, (r_chunk, r_chunk), 0)
            - lax.broadcasted_iota(jnp.int32, (r_chunk, r_chunk), 1)).astype(F32)
    row_k = lax.broadcasted_iota(jnp.int32, (r_chunk, RET_DK), 0).astype(F32)
    for h in range(N_RET_HEADS):
        lg = _LOG_GAMMA[h]
        dmat_ref[h] = jnp.where(diff >= 0.0, jnp.exp(lg * jnp.maximum(diff, 0.0)), 0.0) * _RET_SCALE
        cdec_ref[h] = jnp.exp(lg * (row_k + 1.0))
        kdec_ref[h] = jnp.exp(lg * (r_chunk - 1.0 - row_k)) * _RET_SCALE


def _seq_mix_items(*, vr_ref, gr_ref, qa_ref, qr_ref, kr_ref, kf_ref, sink_ref, a_ref, r_ref, state_ref,
                   bias_ref, pen_ref, dmat_ref, cdec_ref, kdec_ref, first_tile, t_tok, q_blk, r_chunk):
    s_keys = SWA_ROWS + q_blk

    def attention_group(qb, g):
        r0 = qb * q_blk
        k_g = kf_ref[r0:r0 + s_keys, g * HEAD_DIM:(g + 1) * HEAD_DIM]
        v_g = kf_ref[r0:r0 + s_keys, KV_A_DIM + g * HEAD_DIM:KV_A_DIM + (g + 1) * HEAD_DIM]
        q_g = qa_ref[r0:r0 + q_blk, g * GROUP * HEAD_DIM:(g + 1) * GROUP * HEAD_DIM] * jnp.asarray(_ATTN_SCALE, BF16)
        vt_g = v_g.T
        for jh in range(GROUP):
            hd = g * GROUP + jh
            q_h = q_g[:, jh * HEAD_DIM:(jh + 1) * HEAD_DIM]
            s = _dot_nt(k_g, q_h) + bias_ref[hd]
            if pen_ref is not None and qb == 0:
                s = s + pen_ref[first_tile]
            sink = sink_ref[hd]
            m = jnp.maximum(jnp.max(s, axis=0, keepdims=True), sink)
            e = jnp.exp(s - m)
            den = jnp.sum(e, axis=0, keepdims=True) + jnp.exp(sink - m)
            o_t = _dot(vt_g, e.astype(BF16)) * (1.0 / den)
            a_ref[r0:r0 + q_blk, hd * HEAD_DIM:(hd + 1) * HEAD_DIM] = o_t.T.astype(BF16)

    def retention_head(c, h):
        lg = _LOG_GAMMA[h]
        rows = slice(c * r_chunk, (c + 1) * r_chunk)
        q = qr_ref[rows, h * RET_DK:(h + 1) * RET_DK]
        k = kr_ref[rows, h * RET_DK:(h + 1) * RET_DK]
        v = vr_ref[rows, h * RET_DV:(h + 1) * RET_DV]
        st = state_ref[h]
        sc = _dot_nt(q, k) * dmat_ref[h]
        cdec = cdec_ref[h]
        o = _dot(sc.astype(BF16), v) + _dot(q, st.astype(BF16)) * jnp.concatenate([cdec] * (RET_DV // RET_DK), axis=1)
        kd = (k.astype(F32) * kdec_ref[h]).astype(BF16)
        state_ref[h] = st * float(np.exp(np.float32(lg) * np.float32(r_chunk))) + _dot_tn(kd, v)
        c_o = o - jnp.mean(o, axis=-1, keepdims=True)
        nrm = c_o * lax.rsqrt(jnp.mean(c_o * c_o, axis=-1, keepdims=True) + EPS)
        gr = gr_ref[rows, h * RET_DV:(h + 1) * RET_DV].astype(F32)
        r_ref[rows, h * RET_DV:(h + 1) * RET_DV] = (nrm * (gr * _sigmoid(gr))).astype(BF16)

    attn_groups = [(qb, g) for qb in range(t_tok // q_blk) for g in range(N_KV_HEADS)]
    ret_heads = [(c, h) for c in range(t_tok // r_chunk) for h in range(N_RET_HEADS)]
    items = []
    for i in range(max(len(attn_groups), len(ret_heads))):
        if i < len(ret_heads):
            items.append(functools.partial(retention_head, *ret_heads[i]))
        if i < len(attn_groups):
            items.append(functools.partial(attention_group, *attn_groups[i]))
    return items


def _seq_scratch(t_tok, q_blk, r_chunk, with_pen):
    shapes = [
        pltpu.VMEM((N_RET_HEADS, RET_DK, RET_DV), F32),
        pltpu.VMEM((SWA_ROWS + t_tok, KV_COLS), BF16),
        pltpu.VMEM((N_Q_HEADS, SWA_ROWS + q_blk, q_blk), F32),
        pltpu.VMEM((N_RET_HEADS, r_chunk, r_chunk), F32),
        pltpu.VMEM((N_RET_HEADS, r_chunk, RET_DK), F32),
        pltpu.VMEM((N_RET_HEADS, r_chunk, RET_DK), F32),
    ]
    if with_pen:
        shapes.append(pltpu.VMEM((2, SWA_ROWS + q_blk, q_blk), F32))
    return shapes


def _seq_mix_kernel(vr_ref, gr_ref, qa_ref, qr_ref, kr_ref, kv_ref, ck_ref, cv_ref, st_in_ref, sink_ref,
                    a_ref, r_ref, st_out_ref,
                    state_ref, kf_ref, bias_ref, dmat_ref, cdec_ref, kdec_ref, *, t_tok):
    @pl.when(pl.program_id(0) == 0)
    def _():
        _fill_tables(bias_ref, None, dmat_ref, cdec_ref, kdec_ref, q_blk=t_tok, r_chunk=t_tok)

    state_ref[...] = st_in_ref[...]
    kf_ref[0:SWA_ROWS, 0:KV_A_DIM] = ck_ref[...].astype(BF16)
    kf_ref[0:SWA_ROWS, KV_A_DIM:KV_COLS] = cv_ref[...].astype(BF16)
    kf_ref[SWA_ROWS:SWA_ROWS + t_tok, :] = kv_ref[...]
    for item in _seq_mix_items(
            vr_ref=vr_ref, gr_ref=gr_ref, qa_ref=qa_ref, qr_ref=qr_ref, kr_ref=kr_ref, kf_ref=kf_ref,
            sink_ref=sink_ref, a_ref=a_ref, r_ref=r_ref, state_ref=state_ref, bias_ref=bias_ref, pen_ref=None,
            dmat_ref=dmat_ref, cdec_ref=cdec_ref, kdec_ref=kdec_ref, first_tile=None,
            t_tok=t_tok, q_blk=t_tok, r_chunk=t_tok):
        item()
    st_out_ref[...] = state_ref[...]


def _seq_mix(proj_a, proj_b, sinks, cache, *, n_b, t_tok):
    ck, cv, st_in = cache
    m = n_b * t_tok
    return pl.pallas_call(
        functools.partial(_seq_mix_kernel, t_tok=t_tok),
        grid=(n_b,),
        in_specs=[
            pl.BlockSpec((t_tok, RET_V_DIM), lambda b: (b, A_VR)),
            pl.BlockSpec((t_tok, RET_V_DIM), lambda b: (b, A_GR)),
            pl.BlockSpec((t_tok, Q_A_DIM), lambda b: (b, B_QA)),
            pl.BlockSpec((t_tok, RET_QK_DIM), lambda b: (b, B_QR)),
            pl.BlockSpec((t_tok, RET_QK_DIM), lambda b: (b, B_KR)),
            pl.BlockSpec((t_tok, KV_COLS), lambda b: (b, B_KV)),
            pl.BlockSpec((None, SWA_ROWS, KV_A_DIM), lambda b: (b, 0, 0)),
            pl.BlockSpec((None, SWA_ROWS, KV_A_DIM), lambda b: (b, 0, 0)),
            pl.BlockSpec((None, N_RET_HEADS, RET_DK, RET_DV), lambda b: (b, 0, 0, 0)),
            pl.BlockSpec(memory_space=pltpu.SMEM),
        ],
        out_specs=[
            pl.BlockSpec((t_tok, Q_A_DIM), lambda b: (b, 0)),
            pl.BlockSpec((t_tok, RET_V_DIM), lambda b: (b, 0)),
            pl.BlockSpec((None, N_RET_HEADS, RET_DK, RET_DV), lambda b: (b, 0, 0, 0)),
        ],
        out_shape=[
            jax.ShapeDtypeStruct((m, Q_A_DIM), BF16),
            jax.ShapeDtypeStruct((m, RET_V_DIM), BF16),
            jax.ShapeDtypeStruct((n_b, N_RET_HEADS, RET_DK, RET_DV), F32),
        ],
        scratch_shapes=_seq_scratch(t_tok, t_tok, t_tok, with_pen=False),
        compiler_params=_params(1),
        name="seq_mix_sample",
    )(proj_a, proj_a, proj_b, proj_b, proj_b, proj_b, ck, cv, st_in, sinks)


def _merge_math(x, a, r, ga, gb, wau_ref, wru_ref, wout_ref, gpost_ref):
    merged = (_sigmoid(ga.astype(F32)) * _dot(a, wau_ref[...])
              + _sigmoid(gb.astype(F32)) * _dot(r, wru_ref[...]))
    mo = _dot(merged.astype(BF16), wout_ref[...])
    return x + _rms(mo, gpost_ref[...])


def _merge_kernel(x_ref, a_ref, r_ref, ga_ref, gb_ref, wau_ref, wru_ref, wout_ref, gpost_ref, h_ref):
    h_ref[...] = _merge_math(x_ref[...], a_ref[...], r_ref[...], ga_ref[...], gb_ref[...],
                             wau_ref, wru_ref, wout_ref, gpost_ref)


def _merge(x, a, r, proj_a, proj_b, wau, wru, wout, gpost, *, tm):
    m = x.shape[0]
    assert m % tm == 0
    const = lambda shape: pl.BlockSpec(shape, lambda i: (0,) * len(shape), pipeline_mode=pl.Buffered(1))
    return pl.pallas_call(
        _merge_kernel,
        grid=(m // tm,),
        in_specs=[
            pl.BlockSpec((tm, D_MODEL), lambda i: (i, 0)),
            pl.BlockSpec((tm, Q_A_DIM), lambda i: (i, 0)),
            pl.BlockSpec((tm, RET_V_DIM), lambda i: (i, 0)),
            pl.BlockSpec((tm, D_MODEL), lambda i: (i, A_GA)),
            pl.BlockSpec((tm, D_MODEL), lambda i: (i, B_GB)),
            const((Q_A_DIM, D_MODEL)),
            const((RET_V_DIM, D_MODEL)),
            const((D_MODEL, D_MODEL)),
            const((1, D_MODEL)),
        ],
        out_specs=pl.BlockSpec((tm, D_MODEL), lambda i: (i, 0)),
        out_shape=jax.ShapeDtypeStruct((m, D_MODEL), F32),
        compiler_params=_params(1),
        name="merge",
    )(x, a, r, proj_a, proj_b, wau, wru, wout, gpost)


def _mixer_kernel(vr_ref, gr_ref, qa_ref, qr_ref, kr_ref, kv_ref, kvh_ref, sink_ref,
                  x_ref, ga_ref, gb_ref, wau_ref, wru_ref, wout_ref, gpost_ref,
                  h_ref, st_out_ref,
                  state_ref, kf_ref, bias_ref, dmat_ref, cdec_ref, kdec_ref, pen_ref, a_scr, r_scr,
                  *, t_tok, q_blk, r_chunk, n_t, n_tiles):
    s = pl.program_id(0)
    t = lax.rem(jnp.minimum(s, n_tiles - 1), n_t)

    @pl.when(s == 0)
    def _():
        _fill_tables(bias_ref, pen_ref, dmat_ref, cdec_ref, kdec_ref, q_blk=q_blk, r_chunk=r_chunk)
        a_scr[1] = jnp.zeros(a_scr.shape[1:], BF16)
        r_scr[1] = jnp.zeros(r_scr.shape[1:], BF16)

    @pl.when(t == 0)
    def _():
        state_ref[...] = jnp.zeros_like(state_ref)

    kf_ref[0:SWA_ROWS, :] = kvh_ref[...]
    kf_ref[SWA_ROWS:SWA_ROWS + t_tok, :] = kv_ref[...]
    first_tile = jnp.where(t == 0, 1, 0)

    def step(w, r):
        h_ref[...] = _merge_math(x_ref[...], a_scr[r], r_scr[r], ga_ref[...], gb_ref[...],
                                 wau_ref, wru_ref, wout_ref, gpost_ref)
        for item in _seq_mix_items(
                vr_ref=vr_ref, gr_ref=gr_ref, qa_ref=qa_ref, qr_ref=qr_ref, kr_ref=kr_ref, kf_ref=kf_ref,
                sink_ref=sink_ref, a_ref=a_scr.at[w], r_ref=r_scr.at[w], state_ref=state_ref, bias_ref=bias_ref,
                pen_ref=pen_ref, dmat_ref=dmat_ref, cdec_ref=cdec_ref, kdec_ref=kdec_ref, first_tile=first_tile,
                t_tok=t_tok, q_blk=q_blk, r_chunk=r_chunk):
            item()

    parity = lax.rem(s, 2)
    pl.when(parity == 0)(functools.partial(step, 0, 1))
    pl.when(parity == 1)(functools.partial(step, 1, 0))

    @pl.when((t == n_t - 1) & (s < n_tiles))
    def _():
        st_out_ref[...] = state_ref[...]


def _mixer(x, proj_a, proj_b, sinks, wau, wru, wout, gpost, *, n_b, n_t, t_tok, q_blk, r_chunk):
    assert t_tok % q_blk == 0 and t_tok % r_chunk == 0 and q_blk == SWA_ROWS
    n_tiles = n_b * n_t
    halo_per_tile = t_tok // SWA_ROWS
    seq_tile = lambda s: jnp.minimum(s, n_tiles - 1)
    mrg_tile = lambda s: jnp.maximum(s - 1, 0)
    const = lambda shape: pl.BlockSpec(shape, lambda s: (0,) * len(shape), pipeline_mode=pl.Buffered(1))
    return pl.pallas_call(
        functools.partial(_mixer_kernel, t_tok=t_tok, q_blk=q_blk, r_chunk=r_chunk, n_t=n_t, n_tiles=n_tiles),
        grid=(n_tiles + 1,),
        in_specs=[
            pl.BlockSpec((t_tok, RET_V_DIM), lambda s: (seq_tile(s), A_VR)),
            pl.BlockSpec((t_tok, RET_V_DIM), lambda s: (seq_tile(s), A_GR)),
            pl.BlockSpec((t_tok, Q_A_DIM), lambda s: (seq_tile(s), B_QA)),
            pl.BlockSpec((t_tok, RET_QK_DIM), lambda s: (seq_tile(s), B_QR)),
            pl.BlockSpec((t_tok, RET_QK_DIM), lambda s: (seq_tile(s), B_KR)),
            pl.BlockSpec((t_tok, KV_COLS), lambda s: (seq_tile(s), B_KV)),
            pl.BlockSpec((SWA_ROWS, KV_COLS), lambda s: (jnp.maximum(seq_tile(s) * halo_per_tile - 1, 0), B_KV)),
            pl.BlockSpec(memory_space=pltpu.SMEM),
            pl.BlockSpec((t_tok, D_MODEL), lambda s: (mrg_tile(s), 0)),
            pl.BlockSpec((t_tok, D_MODEL), lambda s: (mrg_tile(s), A_GA)),
            pl.BlockSpec((t_tok, D_MODEL), lambda s: (mrg_tile(s), B_GB)),
            const((Q_A_DIM, D_MODEL)),
            const((RET_V_DIM, D_MODEL)),
            const((D_MODEL, D_MODEL)),
            const((1, D_MODEL)),
        ],
        out_specs=[
            pl.BlockSpec((t_tok, D_MODEL), lambda s: (mrg_tile(s), 0)),
            pl.BlockSpec((None, N_RET_HEADS, RET_DK, RET_DV), lambda s: (seq_tile(s) // n_t, 0, 0, 0)),
        ],
        out_shape=[
            jax.ShapeDtypeStruct((n_tiles * t_tok, D_MODEL), F32),
            jax.ShapeDtypeStruct((n_b, N_RET_HEADS, RET_DK, RET_DV), F32),
        ],
        scratch_shapes=_seq_scratch(t_tok, q_blk, r_chunk, with_pen=True) + [
            pltpu.VMEM((2, t_tok, Q_A_DIM), BF16),
            pltpu.VMEM((2, t_tok, RET_V_DIM), BF16),
        ],
        compiler_params=_params(1),
        name="mixer_prompt",
    )(proj_a, proj_a, proj_b, proj_b, proj_b, proj_b, proj_b, sinks, x, proj_a, proj_b, wau, wru, wout, gpost)


def _ffn_kernel(h_ref, wg_ref, wu_ref, wo_ref, gpre_ref, gpost_ref, y_ref, t_ref, *, n_f):
    j = pl.program_id(1)
    tm = t_ref.shape[0]

    def partial_out(rows):
        tt = t_ref[rows, :]
        g = _dot(tt, wg_ref[...])
        up = _dot(tt, wu_ref[...])
        return _dot((g * _sigmoid(g) * up).astype(BF16), wo_ref[...])

    chunks = [slice(c * EDGE_ROWS, (c + 1) * EDGE_ROWS) for c in range(tm // EDGE_ROWS)]

    @pl.when(j == 0)
    def _():
        for rows in chunks:
            t_ref[rows, :] = _rms(h_ref[rows, :], gpre_ref[...]).astype(BF16)
            y_ref[rows, :] = partial_out(rows)

    @pl.when((j > 0) & (j < n_f - 1))
    def _():
        y_ref[...] += partial_out(slice(None))

    @pl.when(j == n_f - 1)
    def _():
        for rows in chunks:
            y_ref[rows, :] = h_ref[rows, :] + _rms(y_ref[rows, :] + partial_out(rows), gpost_ref[...])


def _ffn(h, wff_in, wff_out, gpre, gpost, *, tm, tf):
    m = h.shape[0]
    n_f = D_FF // tf
    assert m % tm == 0 and D_FF % tf == 0 and n_f >= 2 and tm % EDGE_ROWS == 0
    const = lambda shape: pl.BlockSpec(shape, lambda i, j: (0,) * len(shape), pipeline_mode=pl.Buffered(1))
    return pl.pallas_call(
        functools.partial(_ffn_kernel, n_f=n_f),
        grid=(m // tm, n_f),
        in_specs=[
            pl.BlockSpec((tm, D_MODEL), lambda i, j: (i, 0)),
            pl.BlockSpec((D_MODEL, tf), lambda i, j: (0, j)),
            pl.BlockSpec((D_MODEL, tf), lambda i, j: (0, j + n_f)),
            pl.BlockSpec((tf, D_MODEL), lambda i, j: (j, 0)),
            const((1, D_MODEL)),
            const((1, D_MODEL)),
        ],
        out_specs=pl.BlockSpec((tm, D_MODEL), lambda i, j: (i, 0)),
        out_shape=jax.ShapeDtypeStruct((m, D_MODEL), F32),
        scratch_shapes=[pltpu.VMEM((tm, D_MODEL), BF16)],
        compiler_params=_params(vmem_limit_bytes=V7X_VMEM_FFN_LIMIT_BYTES),
        name="ffn",
    )(h, wff_in, wff_in, wff_out, gpre, gpost)


def _ple_kernel(h_ref, p_ref, gple_ref, wpg_ref, wple_ref, y_ref):
    h = h_ref[...]
    gate = _sigmoid(_dot(_rms(h, gple_ref[...]).astype(BF16), wpg_ref[...]))
    y_ref[...] = h + gate * _dot(p_ref[...].astype(BF16), wple_ref[...])


def _ple(h, ple, gple, wpg, wple, *, tm):
    m = h.shape[0]
    assert m % tm == 0
    const = lambda shape: pl.BlockSpec(shape, lambda i: (0,) * len(shape), pipeline_mode=pl.Buffered(1))
    return pl.pallas_call(
        _ple_kernel,
        grid=(m // tm,),
        in_specs=[
            pl.BlockSpec((tm, D_MODEL), lambda i: (i, 0)),
            pl.BlockSpec((tm, PLE_DIM), lambda i: (i, 0)),
            const((1, D_MODEL)),
            const((D_MODEL, D_MODEL)),
            const((PLE_DIM, D_MODEL)),
        ],
        out_specs=pl.BlockSpec((tm, D_MODEL), lambda i: (i, 0)),
        out_shape=jax.ShapeDtypeStruct((m, D_MODEL), F32),
        compiler_params=_params(1),
        name="ple",
    )(h, ple, gple, wpg, wple)


def _layer(x, ple, w, *, n_b, seq, cache=None):
    is_prompt = cache is None
    tm = 1024 if x.shape[0] % 1024 == 0 else 512
    proj_a, proj_b, kv = _in_proj(x, w["g_pre_mix"], *w["w_in"], tm_a=tm, tm_b=tm, n_tiles_a=3, n_tiles_b=2)
    if is_prompt:
        h, state = _mixer(x, proj_a, proj_b, w["sinks"], w["w_a_up"], w["w_r_up"], w["w_out"], w["g_post_mix"],
                          n_b=n_b, n_t=seq // 256, t_tok=256, q_blk=SWA_ROWS, r_chunk=256)
    else:
        a, r, state = _seq_mix(proj_a, proj_b, w["sinks"], cache, n_b=n_b, t_tok=seq)
        h = _merge(x, a, r, proj_a, proj_b, w["w_a_up"], w["w_r_up"], w["w_out"], w["g_post_mix"], tm=512)
    h = _ffn(h, w["w_ffn_in"], w["w_ffn_out"], w["g_pre_ffn"], w["g_post_ffn"], tm=tm, tf=512)
    y = _ple(h, ple, w["g_ple"], w["w_ple_gate"], w["w_ple"], tm=512)
    kv = kv.reshape(n_b, seq, KV_COLS)[:, max(seq - SWA_ROWS, 0):]
    return y, kv[:, :, :KV_A_DIM], kv[:, :, KV_A_DIM:], state


def kernel(x_prompt, x_sample, cache_swa_k, cache_swa_v, state_ret, p_prompt, p_sample, norm_pre_mix, w_in, attn_sinks, w_a_up, w_r_up, w_out, norm_post_mix, norm_pre_ffn, w_ffn_in, w_ffn_out, norm_post_ffn, norm_ple, w_ple_gate, w_ple):
    depth = w_in.shape[0]
    n_bp, seq_p, _ = x_prompt.shape
    n_bs, seq_s, _ = x_sample.shape
    hp = x_prompt.reshape(n_bp * seq_p, D_MODEL)
    hs = x_sample.reshape(n_bs * seq_s, D_MODEL)
    at = (lambda a, l: a.reshape(a.shape[1:])) if depth == 1 else (lambda a, l: a[l])
    outs = [[] for _ in range(6)]
    for l in range(depth):
        w = {
            "w_in": _regroup_w_in(at(w_in, l)),
            "w_a_up": at(w_a_up, l).astype(BF16),
            "w_r_up": at(w_r_up, l).astype(BF16),
            "w_out": at(w_out, l).astype(BF16),
            "w_ffn_in": at(w_ffn_in, l).astype(BF16),
            "w_ffn_out": at(w_ffn_out, l).astype(BF16),
            "w_ple_gate": at(w_ple_gate, l).astype(BF16),
            "w_ple": at(w_ple, l).astype(BF16),
            "sinks": at(attn_sinks, l),
            "g_pre_mix": at(norm_pre_mix, l).reshape(1, D_MODEL),
            "g_post_mix": at(norm_post_mix, l).reshape(1, D_MODEL),
            "g_pre_ffn": at(norm_pre_ffn, l).reshape(1, D_MODEL),
            "g_post_ffn": at(norm_post_ffn, l).reshape(1, D_MODEL),
            "g_ple": at(norm_ple, l).reshape(1, D_MODEL),
        }
        hp, kp, vp, sp = _layer(hp, at(p_prompt, l).reshape(n_bp * seq_p, PLE_DIM), w, n_b=n_bp, seq=seq_p)
        cache = (at(cache_swa_k, l).reshape(n_bs, SWA_ROWS, KV_A_DIM), at(cache_swa_v, l).reshape(n_bs, SWA_ROWS, KV_A_DIM),
                 at(state_ret, l))
        hs, k_s, v_s, s_s = _layer(hs, at(p_sample, l).reshape(n_bs * seq_s, PLE_DIM), w, n_b=n_bs, seq=seq_s, cache=cache)
        kv_shape = (SWA_ROWS, N_KV_HEADS, HEAD_DIM)
        outs[0].append(kp.reshape(n_bp, *kv_shape))
        outs[1].append(vp.reshape(n_bp, *kv_shape))
        outs[2].append(sp)
        outs[3].append(jnp.concatenate([cache[0][:, seq_s:], k_s], axis=1).reshape(n_bs, *kv_shape))
        outs[4].append(jnp.concatenate([cache[1][:, seq_s:], v_s], axis=1).reshape(n_bs, *kv_shape))
        outs[5].append(s_s)
    stack = (lambda o: o[0][None]) if depth == 1 else jnp.stack
    return (hp.reshape(n_bp, seq_p, D_MODEL), hs.reshape(n_bs, seq_s, D_MODEL), *(stack(o) for o in outs))
```

```python
import functools

import jax
import jax.numpy as jnp
import numpy as np
from jax import lax
from jax.experimental import pallas as pl
from jax.experimental.pallas import tpu as pltpu

F32 = jnp.float32
BF16 = jnp.bfloat16

D_MODEL = 2048
CHUNK = 64
PLE_DIM = 256
N_Q_HEADS = 16
N_KV_HEADS = 4
GROUP = N_Q_HEADS // N_KV_HEADS
HEAD_DIM = 64
SWA_ROWS = 128
Q_A_DIM = N_Q_HEADS * HEAD_DIM
KV_A_DIM = N_KV_HEADS * HEAD_DIM
N_RET_HEADS = 8
RET_DK = 128
RET_DV = 256
RET_QK_DIM = N_RET_HEADS * RET_DK
RET_V_DIM = N_RET_HEADS * RET_DV
D_FF = ((8 * D_MODEL // 3 + 255) // 256) * 256
D_IN = Q_A_DIM + 2 * KV_A_DIM + 2 * RET_QK_DIM + 2 * RET_V_DIM + 2 * D_MODEL
NEG_INF = -1e30
EPS = 1e-6

KV_COLS = 2 * KV_A_DIM
PROJ_A_COLS = 2 * RET_V_DIM + D_MODEL
PROJ_B_COLS = D_IN - PROJ_A_COLS
A_VR, A_GR, A_GA = 0, 1, 2
B_GB = 0
B_QA, B_QR, B_KR = 2, 3, 4
B_KV = (PROJ_B_COLS - KV_COLS) // KV_COLS

V7X_VMEM_LIMIT_BYTES = 60 * 1024 * 1024
V7X_VMEM_FFN_LIMIT_BYTES = 63 * 1024 * 1024
EDGE_ROWS = 256

_ALIBI_SLOPES = [float(v) for v in np.exp2(np.float32(-8.0) * (np.arange(N_Q_HEADS, dtype=np.float32) + 1.0) / N_Q_HEADS)]
_LOG_GAMMA = [float(v) for v in np.log(np.float32(1.0) - np.exp2(np.float32(-5.0) - np.arange(N_RET_HEADS, dtype=np.float32)))]
_ATTN_SCALE = HEAD_DIM ** -0.5
_RET_SCALE = RET_DK ** -0.5


def _dot(a, b):
    return jnp.dot(a, b, preferred_element_type=F32)


def _dot_nt(a, b):
    return lax.dot_general(a, b, (((1,), (1,)), ((), ())), preferred_element_type=F32)


def _dot_tn(a, b):
    return lax.dot_general(a, b, (((0,), (0,)), ((), ())), preferred_element_type=F32)


def _rms(x, g):
    return x * lax.rsqrt(jnp.mean(x * x, axis=-1, keepdims=True) + EPS) * g


def _sigmoid(x):
    return 1.0 / (1.0 + jnp.exp(-x))


def _params(grid_rank=2, vmem_limit_bytes=V7X_VMEM_LIMIT_BYTES):
    return pltpu.CompilerParams(dimension_semantics=("arbitrary",) * grid_rank,
                                vmem_limit_bytes=vmem_limit_bytes)


def _regroup_kernel(w_ref, oa_ref, ob_ref, *, n_a):
    j = pl.program_id(0)

    @pl.when(j < n_a)
    def _():
        oa_ref[...] = w_ref[...].astype(BF16)

    @pl.when(j >= n_a)
    def _():
        ob_ref[...] = w_ref[...].astype(BF16)


def _regroup_w_in(w):
    n_tiles = D_IN // KV_COLS
    n_a = PROJ_A_COLS // KV_COLS
    ret_v_tile = (Q_A_DIM + KV_COLS + 2 * RET_QK_DIM) // KV_COLS
    n_wide = n_tiles - ret_v_tile
    n_q = Q_A_DIM // KV_COLS

    def src_tile(j):
        return jnp.where(j < n_wide, j + ret_v_tile,
                         jnp.where(j < n_wide + n_q, j - n_wide,
                                   jnp.where(j < n_tiles - 1, j - n_wide + 1, n_q)))

    return pl.pallas_call(
        functools.partial(_regroup_kernel, n_a=n_a),
        grid=(n_tiles,),
        in_specs=[pl.BlockSpec((D_MODEL, KV_COLS), lambda j: (0, src_tile(j)))],
        out_specs=[
            pl.BlockSpec((D_MODEL, KV_COLS), lambda j: (0, jnp.minimum(j, n_a - 1))),
            pl.BlockSpec((D_MODEL, KV_COLS), lambda j: (0, jnp.maximum(j - n_a, 0))),
        ],
        out_shape=[
            jax.ShapeDtypeStruct((D_MODEL, PROJ_A_COLS), BF16),
            jax.ShapeDtypeStruct((D_MODEL, PROJ_B_COLS), BF16),
        ],
        compiler_params=_params(1),
        name="regroup_w_in",
    )(w)


def _in_proj_a_kernel(x_ref, g_ref, w_ref, o_ref, u_ref):
    j = pl.program_id(1)

    @pl.when(j == 0)
    def _():
        for c in range(u_ref.shape[0] // EDGE_ROWS):
            rows = slice(c * EDGE_ROWS, (c + 1) * EDGE_ROWS)
            u_ref[rows, :] = _rms(x_ref[rows, :], g_ref[...]).astype(BF16)
            o_ref[rows, :] = _dot(u_ref[rows, :], w_ref[...]).astype(BF16)

    @pl.when(j > 0)
    def _():
        o_ref[...] = _dot(u_ref[...], w_ref[...]).astype(BF16)


def _in_proj_b_kernel(u_ref, w_ref, o_ref, kv_ref, *, kv_tile, kv_off):
    acc = _dot(u_ref[...], w_ref[...])
    o_ref[...] = acc.astype(BF16)

    @pl.when(pl.program_id(1) == kv_tile)
    def _():
        kv_ref[...] = acc[:, kv_off:kv_off + KV_COLS]


def _in_proj(x, g, w_a, w_b, *, tm_a, tm_b, n_tiles_a, n_tiles_b):
    m = x.shape[0]
    tn_a, tn_b = PROJ_A_COLS // n_tiles_a, PROJ_B_COLS // n_tiles_b
    kv_start = PROJ_B_COLS - KV_COLS
    kv_tile, kv_off = kv_start // tn_b, kv_start % tn_b
    assert m % tm_a == 0 and m % tm_b == 0 and PROJ_A_COLS % n_tiles_a == 0 and PROJ_B_COLS % n_tiles_b == 0
    assert kv_off + KV_COLS <= tn_b
    proj_a, u = pl.pallas_call(
        _in_proj_a_kernel,
        grid=(m // tm_a, n_tiles_a),
        in_specs=[
            pl.BlockSpec((tm_a, D_MODEL), lambda i, j: (i, 0)),
            pl.BlockSpec((1, D_MODEL), lambda i, j: (0, 0)),
            pl.BlockSpec((D_MODEL, tn_a), lambda i, j: (0, j)),
        ],
        out_specs=[
            pl.BlockSpec((tm_a, tn_a), lambda i, j: (i, j)),
            pl.BlockSpec((tm_a, D_MODEL), lambda i, j: (i, 0)),
        ],
        out_shape=[
            jax.ShapeDtypeStruct((m, PROJ_A_COLS), BF16),
            jax.ShapeDtypeStruct((m, D_MODEL), BF16),
        ],
        compiler_params=_params(),
        name="in_proj_a",
    )(x, g, w_a)
    proj_b, kv = pl.pallas_call(
        functools.partial(_in_proj_b_kernel, kv_tile=kv_tile, kv_off=kv_off),
        grid=(m // tm_b, n_tiles_b),
        in_specs=[
            pl.BlockSpec((tm_b, D_MODEL), lambda i, j: (i, 0)),
            pl.BlockSpec((D_MODEL, tn_b), lambda i, j: (0, j)),
        ],
        out_specs=[
            pl.BlockSpec((tm_b, tn_b), lambda i, j: (i, j)),
            pl.BlockSpec((tm_b, KV_COLS), lambda i, j: (i, 0)),
        ],
        out_shape=[
            jax.ShapeDtypeStruct((m, PROJ_B_COLS), BF16),
            jax.ShapeDtypeStruct((m, KV_COLS), F32),
        ],
        compiler_params=_params(),
        name="in_proj_b",
    )(u, w_b)
    return proj_a, proj_b, kv


def _fill_tables(bias_ref, pen_ref, dmat_ref, cdec_ref, kdec_ref, *, q_blk, r_chunk):
    s_keys = SWA_ROWS + q_blk
    qi = lax.broadcasted_iota(jnp.int32, (s_keys, q_blk), 1) + SWA_ROWS
    ki = lax.broadcasted_iota(jnp.int32, (s_keys, q_blk), 0)
    dist = jnp.abs(qi - ki).astype(F32)
    qc = qi // CHUNK
    kc = ki // CHUNK
    band = (kc >= qc - SWA_ROWS // CHUNK) & (kc <= qc)
    for hd in range(N_Q_HEADS):
        bias_ref[hd] = jnp.where(band, -_ALIBI_SLOPES[hd] * dist, NEG_INF)
    if pen_ref is not None:
        pen_ref[0] = jnp.zeros((s_keys, q_blk), F32)
        pen_ref[1] = jnp.where(ki >= SWA_ROWS, 0.0, NEG_INF)

    diff = (lax.broadcasted_iota(jnp.int32, (r_chunk, r_chunk), 0)
            - lax.broadcasted_iota(jnp.int32, (r_chunk, r_chunk), 1)).astype(F32)
    row_k = lax.broadcasted_iota(jnp.int32, (r_chunk, RET_DK), 0).astype(F32)
    for h in range(N_RET_HEADS):
        lg = _LOG_GAMMA[h]
        dmat_ref[h] = jnp.where(diff >= 0.0, jnp.exp(lg * jnp.maximum(diff, 0.0)), 0.0) * _RET_SCALE
        cdec_ref[h] = jnp.exp(lg * (row_k + 1.0))
        kdec_ref[h] = jnp.exp(lg * (r_chunk - 1.0 - row_k)) * _RET_SCALE


def _seq_mix_items(*, vr_ref, gr_ref, qa_ref, qr_ref, kr_ref, kf_ref, sink_ref, a_ref, r_ref, state_ref,
                   bias_ref, pen_ref, dmat_ref, cdec_ref, kdec_ref, first_tile, t_tok, q_blk, r_chunk):
    s_keys = SWA_ROWS + q_blk

    def attention_group(qb, g):
        r0 = qb * q_blk
        k_g = kf_ref[r0:r0 + s_keys, g * HEAD_DIM:(g + 1) * HEAD_DIM]
        v_g = kf_ref[r0:r0 + s_keys, KV_A_DIM + g * HEAD_DIM:KV_A_DIM + (g + 1) * HEAD_DIM]
        q_g = qa_ref[r0:r0 + q_blk, g * GROUP * HEAD_DIM:(g + 1) * GROUP * HEAD_DIM] * jnp.asarray(_ATTN_SCALE, BF16)
        vt_g = v_g.T
        for jh in range(GROUP):
            hd = g * GROUP + jh
            q_h = q_g[:, jh * HEAD_DIM:(jh + 1) * HEAD_DIM]
            s = _dot_nt(k_g, q_h) + bias_ref[hd]
            if pen_ref is not None and qb == 0:
                s = s + pen_ref[first_tile]
            sink = sink_ref[hd]
            m = jnp.maximum(jnp.max(s, axis=0, keepdims=True), sink)
            e = jnp.exp(s - m)
            den = jnp.sum(e, axis=0, keepdims=True) + jnp.exp(sink - m)
            o_t = _dot(vt_g, e.astype(BF16)) * (1.0 / den)
            a_ref[r0:r0 + q_blk, hd * HEAD_DIM:(hd + 1) * HEAD_DIM] = o_t.T.astype(BF16)

    def retention_head(c, h):
        lg = _LOG_GAMMA[h]
        rows = slice(c * r_chunk, (c + 1) * r_chunk)
        q = qr_ref[rows, h * RET_DK:(h + 1) * RET_DK]
        k = kr_ref[rows, h * RET_DK:(h + 1) * RET_DK]
        v = vr_ref[rows, h * RET_DV:(h + 1) * RET_DV]
        st = state_ref[h]
        sc = _dot_nt(q, k) * dmat_ref[h]
        cdec = cdec_ref[h]
        o = _dot(sc.astype(BF16), v) + _dot(q, st.astype(BF16)) * jnp.concatenate([cdec] * (RET_DV // RET_DK), axis=1)
        kd = (k.astype(F32) * kdec_ref[h]).astype(BF16)
        state_ref[h] = st * float(np.exp(np.float32(lg) * np.float32(r_chunk))) + _dot_tn(kd, v)
        c_o = o - jnp.mean(o, axis=-1, keepdims=True)
        nrm = c_o * lax.rsqrt(jnp.mean(c_o * c_o, axis=-1, keepdims=True) + EPS)
        gr = gr_ref[rows, h * RET_DV:(h + 1) * RET_DV].astype(F32)
        r_ref[rows, h * RET_DV:(h + 1) * RET_DV] = (nrm * (gr * _sigmoid(gr))).astype(BF16)

    attn_groups = [(qb, g) for qb in range(t_tok // q_blk) for g in range(N_KV_HEADS)]
    ret_heads = [(c, h) for c in range(t_tok // r_chunk) for h in range(N_RET_HEADS)]
    items = []
    for i in range(max(len(attn_groups), len(ret_heads))):
        if i < len(ret_heads):
            items.append(functools.partial(retention_head, *ret_heads[i]))
        if i < len(attn_groups):
            items.append(functools.partial(attention_group, *attn_groups[i]))
    return items


def _seq_scratch(t_tok, q_blk, r_chunk, with_pen):
    shapes = [
        pltpu.VMEM((N_RET_HEADS, RET_DK, RET_DV), F32),
        pltpu.VMEM((SWA_ROWS + t_tok, KV_COLS), BF16),
        pltpu.VMEM((N_Q_HEADS, SWA_ROWS + q_blk, q_blk), F32),
        pltpu.VMEM((N_RET_HEADS, r_chunk, r_chunk), F32),
        pltpu.VMEM((N_RET_HEADS, r_chunk, RET_DK), F32),
        pltpu.VMEM((N_RET_HEADS, r_chunk, RET_DK), F32),
    ]
    if with_pen:
        shapes.append(pltpu.VMEM((2, SWA_ROWS + q_blk, q_blk), F32))
    return shapes


def _seq_mix_kernel(vr_ref, gr_ref, qa_ref, qr_ref, kr_ref, kv_ref, ck_ref, cv_ref, st_in_ref, sink_ref,
                    a_ref, r_ref, st_out_ref,
                    state_ref, kf_ref, bias_ref, dmat_ref, cdec_ref, kdec_ref, *, t_tok):
    @pl.when(pl.program_id(0) == 0)
    def _():
        _fill_tables(bias_ref, None, dmat_ref, cdec_ref, kdec_ref, q_blk=t_tok, r_chunk=t_tok)

    state_ref[...] = st_in_ref[...]
    kf_ref[0:SWA_ROWS, 0:KV_A_DIM] = ck_ref[...].astype(BF16)
    kf_ref[0:SWA_ROWS, KV_A_DIM:KV_COLS] = cv_ref[...].astype(BF16)
    kf_ref[SWA_ROWS:SWA_ROWS + t_tok, :] = kv_ref[...]
    for item in _seq_mix_items(
            vr_ref=vr_ref, gr_ref=gr_ref, qa_ref=qa_ref, qr_ref=qr_ref, kr_ref=kr_ref, kf_ref=kf_ref,
            sink_ref=sink_ref, a_ref=a_ref, r_ref=r_ref, state_ref=state_ref, bias_ref=bias_ref, pen_ref=None,
            dmat_ref=dmat_ref, cdec_ref=cdec_ref, kdec_ref=kdec_ref, first_tile=None,
            t_tok=t_tok, q_blk=t_tok, r_chunk=t_tok):
        item()
    st_out_ref[...] = state_ref[...]


def _seq_mix(proj_a, proj_b, sinks, cache, *, n_b, t_tok):
    ck, cv, st_in = cache
    m = n_b * t_tok
    return pl.pallas_call(
        functools.partial(_seq_mix_kernel, t_tok=t_tok),
        grid=(n_b,),
        in_specs=[
            pl.BlockSpec((t_tok, RET_V_DIM), lambda b: (b, A_VR)),
            pl.BlockSpec((t_tok, RET_V_DIM), lambda b: (b, A_GR)),
            pl.BlockSpec((t_tok, Q_A_DIM), lambda b: (b, B_QA)),
            pl.BlockSpec((t_tok, RET_QK_DIM), lambda b: (b, B_QR)),
            pl.BlockSpec((t_tok, RET_QK_DIM), lambda b: (b, B_KR)),
            pl.BlockSpec((t_tok, KV_COLS), lambda b: (b, B_KV)),
            pl.BlockSpec((None, SWA_ROWS, KV_A_DIM), lambda b: (b, 0, 0)),
            pl.BlockSpec((None, SWA_ROWS, KV_A_DIM), lambda b: (b, 0, 0)),
            pl.BlockSpec((None, N_RET_HEADS, RET_DK, RET_DV), lambda b: (b, 0, 0, 0)),
            pl.BlockSpec(memory_space=pltpu.SMEM),
        ],
        out_specs=[
            pl.BlockSpec((t_tok, Q_A_DIM), lambda b: (b, 0)),
            pl.BlockSpec((t_tok, RET_V_DIM), lambda b: (b, 0)),
            pl.BlockSpec((None, N_RET_HEADS, RET_DK, RET_DV), lambda b: (b, 0, 0, 0)),
        ],
        out_shape=[
            jax.ShapeDtypeStruct((m, Q_A_DIM), BF16),
            jax.ShapeDtypeStruct((m, RET_V_DIM), BF16),
            jax.ShapeDtypeStruct((n_b, N_RET_HEADS, RET_DK, RET_DV), F32),
        ],
        scratch_shapes=_seq_scratch(t_tok, t_tok, t_tok, with_pen=False),
        compiler_params=_params(1),
        name="seq_mix_sample",
    )(proj_a, proj_a, proj_b, proj_b, proj_b, proj_b, ck, cv, st_in, sinks)


def _merge_math(x, a, r, ga, gb, wau_ref, wru_ref, wout_ref, gpost_ref):
    merged = (_sigmoid(ga.astype(F32)) * _dot(a, wau_ref[...])
              + _sigmoid(gb.astype(F32)) * _dot(r, wru_ref[...]))
    mo = _dot(merged.astype(BF16), wout_ref[...])
    return x + _rms(mo, gpost_ref[...])


def _merge_kernel(x_ref, a_ref, r_ref, ga_ref, gb_ref, wau_ref, wru_ref, wout_ref, gpost_ref, h_ref):
    h_ref[...] = _merge_math(x_ref[...], a_ref[...], r_ref[...], ga_ref[...], gb_ref[...],
                             wau_ref, wru_ref, wout_ref, gpost_ref)


def _merge(x, a, r, proj_a, proj_b, wau, wru, wout, gpost, *, tm):
    m = x.shape[0]
    assert m % tm == 0
    const = lambda shape: pl.BlockSpec(shape, lambda i: (0,) * len(shape), pipeline_mode=pl.Buffered(1))
    return pl.pallas_call(
        _merge_kernel,
        grid=(m // tm,),
        in_specs=[
            pl.BlockSpec((tm, D_MODEL), lambda i: (i, 0)),
            pl.BlockSpec((tm, Q_A_DIM), lambda i: (i, 0)),
            pl.BlockSpec((tm, RET_V_DIM), lambda i: (i, 0)),
            pl.BlockSpec((tm, D_MODEL), lambda i: (i, A_GA)),
            pl.BlockSpec((tm, D_MODEL), lambda i: (i, B_GB)),
            const((Q_A_DIM, D_MODEL)),
            const((RET_V_DIM, D_MODEL)),
            const((D_MODEL, D_MODEL)),
            const((1, D_MODEL)),
        ],
        out_specs=pl.BlockSpec((tm, D_MODEL), lambda i: (i, 0)),
        out_shape=jax.ShapeDtypeStruct((m, D_MODEL), F32),
        compiler_params=_params(1),
        name="merge",
    )(x, a, r, proj_a, proj_b, wau, wru, wout, gpost)


def _mixer_kernel(vr_ref, gr_ref, qa_ref, qr_ref, kr_ref, kv_ref, kvh_ref, sink_ref,
                  x_ref, ga_ref, gb_ref, wau_ref, wru_ref, wout_ref, gpost_ref,
                  h_ref, st_out_ref,
                  state_ref, kf_ref, bias_ref, dmat_ref, cdec_ref, kdec_ref, pen_ref, a_scr, r_scr,
                  *, t_tok, q_blk, r_chunk, n_t, n_tiles):
    s = pl.program_id(0)
    t = lax.rem(jnp.minimum(s, n_tiles - 1), n_t)

    @pl.when(s == 0)
    def _():
        _fill_tables(bias_ref, pen_ref, dmat_ref, cdec_ref, kdec_ref, q_blk=q_blk, r_chunk=r_chunk)
        a_scr[1] = jnp.zeros(a_scr.shape[1:], BF16)
        r_scr[1] = jnp.zeros(r_scr.shape[1:], BF16)

    @pl.when(t == 0)
    def _():
        state_ref[...] = jnp.zeros_like(state_ref)

    kf_ref[0:SWA_ROWS, :] = kvh_ref[...]
    kf_ref[SWA_ROWS:SWA_ROWS + t_tok, :] = kv_ref[...]
    first_tile = jnp.where(t == 0, 1, 0)

    def step(w, r):
        h_ref[...] = _merge_math(x_ref[...], a_scr[r], r_scr[r], ga_ref[...], gb_ref[...],
                                 wau_ref, wru_ref, wout_ref, gpost_ref)
        for item in _seq_mix_items(
                vr_ref=vr_ref, gr_ref=gr_ref, qa_ref=qa_ref, qr_ref=qr_ref, kr_ref=kr_ref, kf_ref=kf_ref,
                sink_ref=sink_ref, a_ref=a_scr.at[w], r_ref=r_scr.at[w], state_ref=state_ref, bias_ref=bias_ref,
                pen_ref=pen_ref, dmat_ref=dmat_ref, cdec_ref=cdec_ref, kdec_ref=kdec_ref, first_tile=first_tile,
                t_tok=t_tok, q_blk=q_blk, r_chunk=r_chunk):
            item()

    w = lax.rem(s, 2)
    step(w, 1 - w)

    @pl.when((t == n_t - 1) & (s < n_tiles))
    def _():
        st_out_ref[...] = state_ref[...]


def _mixer(x, proj_a, proj_b, sinks, wau, wru, wout, gpost, *, n_b, n_t, t_tok, q_blk, r_chunk):
    assert t_tok % q_blk == 0 and t_tok % r_chunk == 0 and q_blk == SWA_ROWS
    n_tiles = n_b * n_t
    halo_per_tile = t_tok // SWA_ROWS
    seq_tile = lambda s: jnp.minimum(s, n_tiles - 1)
    mrg_tile = lambda s: jnp.maximum(s - 1, 0)
    const = lambda shape: pl.BlockSpec(shape, lambda s: (0,) * len(shape), pipeline_mode=pl.Buffered(1))
    return pl.pallas_call(
        functools.partial(_mixer_kernel, t_tok=t_tok, q_blk=q_blk, r_chunk=r_chunk, n_t=n_t, n_tiles=n_tiles),
        grid=(n_tiles + 1,),
        in_specs=[
            pl.BlockSpec((t_tok, RET_V_DIM), lambda s: (seq_tile(s), A_VR)),
            pl.BlockSpec((t_tok, RET_V_DIM), lambda s: (seq_tile(s), A_GR)),
            pl.BlockSpec((t_tok, Q_A_DIM), lambda s: (seq_tile(s), B_QA)),
            pl.BlockSpec((t_tok, RET_QK_DIM), lambda s: (seq_tile(s), B_QR)),
            pl.BlockSpec((t_tok, RET_QK_DIM), lambda s: (seq_tile(s), B_KR)),
            pl.BlockSpec((t_tok, KV_COLS), lambda s: (seq_tile(s), B_KV)),
            pl.BlockSpec((SWA_ROWS, KV_COLS), lambda s: (jnp.maximum(seq_tile(s) * halo_per_tile - 1, 0), B_KV)),
            pl.BlockSpec(memory_space=pltpu.SMEM),
            pl.BlockSpec((t_tok, D_MODEL), lambda s: (mrg_tile(s), 0)),
            pl.BlockSpec((t_tok, D_MODEL), lambda s: (mrg_tile(s), A_GA)),
            pl.BlockSpec((t_tok, D_MODEL), lambda s: (mrg_tile(s), B_GB)),
            const((Q_A_DIM, D_MODEL)),
            const((RET_V_DIM, D_MODEL)),
            const((D_MODEL, D_MODEL)),
            const((1, D_MODEL)),
        ],
        out_specs=[
            pl.BlockSpec((t_tok, D_MODEL), lambda s: (mrg_tile(s), 0)),
            pl.BlockSpec((None, N_RET_HEADS, RET_DK, RET_DV), lambda s: (seq_tile(s) // n_t, 0, 0, 0)),
        ],
        out_shape=[
            jax.ShapeDtypeStruct((n_tiles * t_tok, D_MODEL), F32),
            jax.ShapeDtypeStruct((n_b, N_RET_HEADS, RET_DK, RET_DV), F32),
        ],
        scratch_shapes=_seq_scratch(t_tok, q_blk, r_chunk, with_pen=True) + [
            pltpu.VMEM((2, t_tok, Q_A_DIM), BF16),
            pltpu.VMEM((2, t_tok, RET_V_DIM), BF16),
        ],
        compiler_params=_params(1),
        name="mixer_prompt",
    )(proj_a, proj_a, proj_b, proj_b, proj_b, proj_b, proj_b, sinks, x, proj_a, proj_b, wau, wru, wout, gpost)


def _ffn_kernel(h_ref, wg_ref, wu_ref, wo_ref, gpre_ref, gpost_ref, y_ref, t_ref, *, n_f):
    j = pl.program_id(1)
    tm = t_ref.shape[0]

    def partial_out(rows):
        tt = t_ref[rows, :]
        g = _dot(tt, wg_ref[...])
        up = _dot(tt, wu_ref[...])
        return _dot((g * _sigmoid(g) * up).astype(BF16), wo_ref[...])

    chunks = [slice(c * EDGE_ROWS, (c + 1) * EDGE_ROWS) for c in range(tm // EDGE_ROWS)]

    @pl.when(j == 0)
    def _():
        for rows in chunks:
            t_ref[rows, :] = _rms(h_ref[rows, :], gpre_ref[...]).astype(BF16)
            y_ref[rows, :] = partial_out(rows)

    @pl.when((j > 0) & (j < n_f - 1))
    def _():
        y_ref[...] += partial_out(slice(None))

    @pl.when(j == n_f - 1)
    def _():
        for rows in chunks:
            y_ref[rows, :] = h_ref[rows, :] + _rms(y_ref[rows, :] + partial_out(rows), gpost_ref[...])


def _ffn(h, wff_in, wff_out, gpre, gpost, *, tm, tf):
    m = h.shape[0]
    n_f = D_FF // tf
    assert m % tm == 0 and D_FF % tf == 0 and n_f >= 2 and tm % EDGE_ROWS == 0
    const = lambda shape: pl.BlockSpec(shape, lambda i, j: (0,) * len(shape), pipeline_mode=pl.Buffered(1))
    return pl.pallas_call(
        functools.partial(_ffn_kernel, n_f=n_f),
        grid=(m // tm, n_f),
        in_specs=[
            pl.BlockSpec((tm, D_MODEL), lambda i, j: (i, 0)),
            pl.BlockSpec((D_MODEL, tf), lambda i, j: (0, j)),
            pl.BlockSpec((D_MODEL, tf), lambda i, j: (0, j + n_f)),
            pl.BlockSpec((tf, D_MODEL), lambda i, j: (j, 0)),
            const((1, D_MODEL)),
            const((1, D_MODEL)),
        ],
        out_specs=pl.BlockSpec((tm, D_MODEL), lambda i, j: (i, 0)),
        out_shape=jax.ShapeDtypeStruct((m, D_MODEL), F32),
        scratch_shapes=[pltpu.VMEM((tm, D_MODEL), BF16)],
        compiler_params=_params(vmem_limit_bytes=V7X_VMEM_FFN_LIMIT_BYTES),
        name="ffn",
    )(h, wff_in, wff_in, wff_out, gpre, gpost)


def _ple_kernel(h_ref, p_ref, gple_ref, wpg_ref, wple_ref, y_ref):
    for c in range(h_ref.shape[0] // EDGE_ROWS):
        rows = slice(c * EDGE_ROWS, (c + 1) * EDGE_ROWS)
        h = h_ref[rows, :]
        gate = _sigmoid(_dot(_rms(h, gple_ref[...]).astype(BF16), wpg_ref[...]))
        y_ref[rows, :] = h + gate * _dot(p_ref[rows, :].astype(BF16), wple_ref[...])


def _ple(h, ple, gple, wpg, wple, *, tm):
    m = h.shape[0]
    assert m % tm == 0 and tm % EDGE_ROWS == 0
    const = lambda shape: pl.BlockSpec(shape, lambda i: (0,) * len(shape), pipeline_mode=pl.Buffered(1))
    return pl.pallas_call(
        _ple_kernel,
        grid=(m // tm,),
        in_specs=[
            pl.BlockSpec((tm, D_MODEL), lambda i: (i, 0)),
            pl.BlockSpec((tm, PLE_DIM), lambda i: (i, 0)),
            const((1, D_MODEL)),
            const((D_MODEL, D_MODEL)),
            const((PLE_DIM, D_MODEL)),
        ],
        out_specs=pl.BlockSpec((tm, D_MODEL), lambda i: (i, 0)),
        out_shape=jax.ShapeDtypeStruct((m, D_MODEL), F32),
        compiler_params=_params(1),
        name="ple",
    )(h, ple, gple, wpg, wple)


def _layer(x, ple, w, *, n_b, seq, cache=None):
    is_prompt = cache is None
    tm = 1024 if x.shape[0] % 1024 == 0 else 512
    proj_a, proj_b, kv = _in_proj(x, w["g_pre_mix"], *w["w_in"], tm_a=tm, tm_b=tm, n_tiles_a=3, n_tiles_b=2)
    if is_prompt:
        h, state = _mixer(x, proj_a, proj_b, w["sinks"], w["w_a_up"], w["w_r_up"], w["w_out"], w["g_post_mix"],
                          n_b=n_b, n_t=seq // 256, t_tok=256, q_blk=SWA_ROWS, r_chunk=256)
    else:
        a, r, state = _seq_mix(proj_a, proj_b, w["sinks"], cache, n_b=n_b, t_tok=seq)
        h = _merge(x, a, r, proj_a, proj_b, w["w_a_up"], w["w_r_up"], w["w_out"], w["g_post_mix"], tm=512)
    h = _ffn(h, w["w_ffn_in"], w["w_ffn_out"], w["g_pre_ffn"], w["g_post_ffn"], tm=tm, tf=512)
    y = _ple(h, ple, w["g_ple"], w["w_ple_gate"], w["w_ple"], tm=tm)
    kv = kv.reshape(n_b, seq, KV_COLS)[:, max(seq - SWA_ROWS, 0):]
    return y, kv[:, :, :KV_A_DIM], kv[:, :, KV_A_DIM:], state


def kernel(x_prompt, x_sample, cache_swa_k, cache_swa_v, state_ret, p_prompt, p_sample, norm_pre_mix, w_in, attn_sinks, w_a_up, w_r_up, w_out, norm_post_mix, norm_pre_ffn, w_ffn_in, w_ffn_out, norm_post_ffn, norm_ple, w_ple_gate, w_ple):
    depth = w_in.shape[0]
    n_bp, seq_p, _ = x_prompt.shape
    n_bs, seq_s, _ = x_sample.shape
    hp = x_prompt.reshape(n_bp * seq_p, D_MODEL)
    hs = x_sample.reshape(n_bs * seq_s, D_MODEL)
    at = (lambda a, l: a.reshape(a.shape[1:])) if depth == 1 else (lambda a, l: a[l])
    outs = [[] for _ in range(6)]
    for l in range(depth):
        w = {
            "w_in": _regroup_w_in(at(w_in, l)),
            "w_a_up": at(w_a_up, l).astype(BF16),
            "w_r_up": at(w_r_up, l).astype(BF16),
            "w_out": at(w_out, l).astype(BF16),
            "w_ffn_in": at(w_ffn_in, l).astype(BF16),
            "w_ffn_out": at(w_ffn_out, l).astype(BF16),
            "w_ple_gate": at(w_ple_gate, l).astype(BF16),
            "w_ple": at(w_ple, l).astype(BF16),
            "sinks": at(attn_sinks, l),
            "g_pre_mix": at(norm_pre_mix, l).reshape(1, D_MODEL),
            "g_post_mix": at(norm_post_mix, l).reshape(1, D_MODEL),
            "g_pre_ffn": at(norm_pre_ffn, l).reshape(1, D_MODEL),
            "g_post_ffn": at(norm_post_ffn, l).reshape(1, D_MODEL),
            "g_ple": at(norm_ple, l).reshape(1, D_MODEL),
        }
        hp, kp, vp, sp = _layer(hp, at(p_prompt, l).reshape(n_bp * seq_p, PLE_DIM), w, n_b=n_bp, seq=seq_p)
        cache = (at(cache_swa_k, l).reshape(n_bs, SWA_ROWS, KV_A_DIM), at(cache_swa_v, l).reshape(n_bs, SWA_ROWS, KV_A_DIM),
                 at(state_ret, l))
        hs, k_s, v_s, s_s = _layer(hs, at(p_sample, l).reshape(n_bs * seq_s, PLE_DIM), w, n_b=n_bs, seq=seq_s, cache=cache)
        kv_shape = (SWA_ROWS, N_KV_HEADS, HEAD_DIM)
        outs[0].append(kp.reshape(n_bp, *kv_shape))
        outs[1].append(vp.reshape(n_bp, *kv_shape))
        outs[2].append(sp)
        outs[3].append(jnp.concatenate([cache[0][:, seq_s:], k_s], axis=1).reshape(n_bs, *kv_shape))
        outs[4].append(jnp.concatenate([cache[1][:, seq_s:], v_s], axis=1).reshape(n_bs, *kv_shape))
        outs[5].append(s_s)
    stack = (lambda o: o[0][None]) if depth == 1 else jnp.stack
    return (hp.reshape(n_bp, seq_p, D_MODEL), hs.reshape(n_bs, seq_s, D_MODEL), *(stack(o) for o in outs))
```

```python
import functools

import jax
import jax.numpy as jnp
import numpy as np
from jax import lax
from jax.experimental import pallas as pl
from jax.experimental.pallas import tpu as pltpu

F32 = jnp.float32
BF16 = jnp.bfloat16

D_MODEL = 2048
CHUNK = 64
PLE_DIM = 256
N_Q_HEADS = 16
N_KV_HEADS = 4
GROUP = N_Q_HEADS // N_KV_HEADS
HEAD_DIM = 64
SWA_ROWS = 128
Q_A_DIM = N_Q_HEADS * HEAD_DIM
KV_A_DIM = N_KV_HEADS * HEAD_DIM
N_RET_HEADS = 8
RET_DK = 128
RET_DV = 256
RET_QK_DIM = N_RET_HEADS * RET_DK
RET_V_DIM = N_RET_HEADS * RET_DV
D_FF = ((8 * D_MODEL // 3 + 255) // 256) * 256
D_IN = Q_A_DIM + 2 * KV_A_DIM + 2 * RET_QK_DIM + 2 * RET_V_DIM + 2 * D_MODEL
NEG_INF = -1e30
EPS = 1e-6

KV_COLS = 2 * KV_A_DIM
PROJ_A_COLS = 2 * RET_V_DIM + D_MODEL
PROJ_B_COLS = D_IN - PROJ_A_COLS
A_VR, A_GR, A_GA = 0, 1, 2
B_GB = 0
B_QA, B_QR, B_KR = 2, 3, 4
B_KV = (PROJ_B_COLS - KV_COLS) // KV_COLS

V7X_VMEM_LIMIT_BYTES = 60 * 1024 * 1024
V7X_VMEM_FFN_LIMIT_BYTES = 63 * 1024 * 1024
EDGE_ROWS = 256

_ALIBI_SLOPES = [float(v) for v in np.exp2(np.float32(-8.0) * (np.arange(N_Q_HEADS, dtype=np.float32) + 1.0) / N_Q_HEADS)]
_LOG_GAMMA = [float(v) for v in np.log(np.float32(1.0) - np.exp2(np.float32(-5.0) - np.arange(N_RET_HEADS, dtype=np.float32)))]
_ATTN_SCALE = HEAD_DIM ** -0.5
_RET_SCALE = RET_DK ** -0.5


def _dot(a, b):
    return jnp.dot(a, b, preferred_element_type=F32)


def _dot_nt(a, b):
    return lax.dot_general(a, b, (((1,), (1,)), ((), ())), preferred_element_type=F32)


def _dot_tn(a, b):
    return lax.dot_general(a, b, (((0,), (0,)), ((), ())), preferred_element_type=F32)


def _rms(x, g):
    return x * lax.rsqrt(jnp.mean(x * x, axis=-1, keepdims=True) + EPS) * g


def _sigmoid(x):
    return 1.0 / (1.0 + jnp.exp(-x))


def _params(grid_rank=2, vmem_limit_bytes=V7X_VMEM_LIMIT_BYTES):
    return pltpu.CompilerParams(dimension_semantics=("arbitrary",) * grid_rank,
                                vmem_limit_bytes=vmem_limit_bytes)


def _regroup_kernel(w_ref, oa_ref, ob_ref, *, n_a):
    j = pl.program_id(0)

    @pl.when(j < n_a)
    def _():
        oa_ref[...] = w_ref[...].astype(BF16)

    @pl.when(j >= n_a)
    def _():
        ob_ref[...] = w_ref[...].astype(BF16)


def _regroup_w_in(w):
    n_tiles = D_IN // KV_COLS
    n_a = PROJ_A_COLS // KV_COLS
    ret_v_tile = (Q_A_DIM + KV_COLS + 2 * RET_QK_DIM) // KV_COLS
    n_wide = n_tiles - ret_v_tile
    n_q = Q_A_DIM // KV_COLS

    def src_tile(j):
        return jnp.where(j < n_wide, j + ret_v_tile,
                         jnp.where(j < n_wide + n_q, j - n_wide,
                                   jnp.where(j < n_tiles - 1, j - n_wide + 1, n_q)))

    return pl.pallas_call(
        functools.partial(_regroup_kernel, n_a=n_a),
        grid=(n_tiles,),
        in_specs=[pl.BlockSpec((D_MODEL, KV_COLS), lambda j: (0, src_tile(j)))],
        out_specs=[
            pl.BlockSpec((D_MODEL, KV_COLS), lambda j: (0, jnp.minimum(j, n_a - 1))),
            pl.BlockSpec((D_MODEL, KV_COLS), lambda j: (0, jnp.maximum(j - n_a, 0))),
        ],
        out_shape=[
            jax.ShapeDtypeStruct((D_MODEL, PROJ_A_COLS), BF16),
            jax.ShapeDtypeStruct((D_MODEL, PROJ_B_COLS), BF16),
        ],
        compiler_params=_params(1),
        name="regroup_w_in",
    )(w)


def _in_proj_a_kernel(x_ref, g_ref, w_ref, o_ref, u_ref):
    j = pl.program_id(1)

    @pl.when(j == 0)
    def _():
        for c in range(u_ref.shape[0] // EDGE_ROWS):
            rows = slice(c * EDGE_ROWS, (c + 1) * EDGE_ROWS)
            u_ref[rows, :] = _rms(x_ref[rows, :], g_ref[...]).astype(BF16)
            o_ref[rows, :] = _dot(u_ref[rows, :], w_ref[...]).astype(BF16)

    @pl.when(j > 0)
    def _():
        o_ref[...] = _dot(u_ref[...], w_ref[...]).astype(BF16)


def _in_proj_b_kernel(u_ref, w_ref, o_ref, kv_ref, *, kv_tile, kv_off):
    acc = _dot(u_ref[...], w_ref[...])
    o_ref[...] = acc.astype(BF16)

    @pl.when(pl.program_id(1) == kv_tile)
    def _():
        kv_ref[...] = acc[:, kv_off:kv_off + KV_COLS]


def _in_proj(x, g, w_a, w_b, *, tm_a, tm_b, n_tiles_a, n_tiles_b):
    m = x.shape[0]
    tn_a, tn_b = PROJ_A_COLS // n_tiles_a, PROJ_B_COLS // n_tiles_b
    kv_start = PROJ_B_COLS - KV_COLS
    kv_tile, kv_off = kv_start // tn_b, kv_start % tn_b
    assert m % tm_a == 0 and m % tm_b == 0 and PROJ_A_COLS % n_tiles_a == 0 and PROJ_B_COLS % n_tiles_b == 0
    assert kv_off + KV_COLS <= tn_b
    proj_a, u = pl.pallas_call(
        _in_proj_a_kernel,
        grid=(m // tm_a, n_tiles_a),
        in_specs=[
            pl.BlockSpec((tm_a, D_MODEL), lambda i, j: (i, 0)),
            pl.BlockSpec((1, D_MODEL), lambda i, j: (0, 0)),
            pl.BlockSpec((D_MODEL, tn_a), lambda i, j: (0, j)),
        ],
        out_specs=[
            pl.BlockSpec((tm_a, tn_a), lambda i, j: (i, j)),
            pl.BlockSpec((tm_a, D_MODEL), lambda i, j: (i, 0)),
        ],
        out_shape=[
            jax.ShapeDtypeStruct((m, PROJ_A_COLS), BF16),
            jax.ShapeDtypeStruct((m, D_MODEL), BF16),
        ],
        compiler_params=_params(),
        name="in_proj_a",
    )(x, g, w_a)
    proj_b, kv = pl.pallas_call(
        functools.partial(_in_proj_b_kernel, kv_tile=kv_tile, kv_off=kv_off),
        grid=(m // tm_b, n_tiles_b),
        in_specs=[
            pl.BlockSpec((tm_b, D_MODEL), lambda i, j: (i, 0)),
            pl.BlockSpec((D_MODEL, tn_b), lambda i, j: (0, j)),
        ],
        out_specs=[
            pl.BlockSpec((tm_b, tn_b), lambda i, j: (i, j)),
            pl.BlockSpec((tm_b, KV_COLS), lambda i, j: (i, 0)),
        ],
        out_shape=[
            jax.ShapeDtypeStruct((m, PROJ_B_COLS), BF16),
            jax.ShapeDtypeStruct((m, KV_COLS), F32),
        ],
        compiler_params=_params(),
        name="in_proj_b",
    )(u, w_b)
    return proj_a, proj_b, kv


def _fill_tables(bias_ref, pen_ref, dmat_ref, cdec_ref, kdec_ref, *, q_blk, r_chunk):
    s_keys = SWA_ROWS + q_blk
    qi = lax.broadcasted_iota(jnp.int32, (s_keys, q_blk), 1) + SWA_ROWS
    ki = lax.broadcasted_iota(jnp.int32, (s_keys, q_blk), 0)
    dist = jnp.abs(qi - ki).astype(F32)
    qc = qi // CHUNK
    kc = ki // CHUNK
    band = (kc >= qc - SWA_ROWS // CHUNK) & (kc <= qc)
    for hd in range(N_Q_HEADS):
        bias_ref[hd] = jnp.where(band, -_ALIBI_SLOPES[hd] * dist, NEG_INF)
    if pen_ref is not None:
        pen_ref[0] = jnp.zeros((s_keys, q_blk), F32)
        pen_ref[1] = jnp.where(ki >= SWA_ROWS, 0.0, NEG_INF)

    diff = (lax.broadcasted_iota(jnp.int32, (r_chunk, r_chunk), 0)
            - lax.broadcasted_iota(jnp.int32, (r_chunk, r_chunk), 1)).astype(F32)
    row_k = lax.broadcasted_iota(jnp.int32, (r_chunk, RET_DK), 0).astype(F32)
    for h in range(N_RET_HEADS):
        lg = _LOG_GAMMA[h]
        dmat_ref[h] = jnp.where(diff >= 0.0, jnp.exp(lg * jnp.maximum(diff, 0.0)), 0.0) * _RET_SCALE
        cdec_ref[h] = jnp.exp(lg * (row_k + 1.0))
        kdec_ref[h] = jnp.exp(lg * (r_chunk - 1.0 - row_k)) * _RET_SCALE


def _seq_mix_items(*, vr_ref, qa_ref, qr_ref, kr_ref, kf_ref, sink_ref, a_ref, r_ref, state_ref,
                   bias_ref, pen_ref, dmat_ref, cdec_ref, kdec_ref, first_tile, t_tok, q_blk, r_chunk):
    s_keys = SWA_ROWS + q_blk

    def attention_group(qb, g):
        r0 = qb * q_blk
        k_g = kf_ref[r0:r0 + s_keys, g * HEAD_DIM:(g + 1) * HEAD_DIM]
        v_g = kf_ref[r0:r0 + s_keys, KV_A_DIM + g * HEAD_DIM:KV_A_DIM + (g + 1) * HEAD_DIM]
        q_g = qa_ref[r0:r0 + q_blk, g * GROUP * HEAD_DIM:(g + 1) * GROUP * HEAD_DIM] * jnp.asarray(_ATTN_SCALE, BF16)
        vt_g = v_g.T
        for jh in range(GROUP):
            hd = g * GROUP + jh
            q_h = q_g[:, jh * HEAD_DIM:(jh + 1) * HEAD_DIM]
            s = _dot_nt(k_g, q_h) + bias_ref[hd]
            if pen_ref is not None and qb == 0:
                s = s + pen_ref[first_tile]
            sink = sink_ref[hd]
            m = jnp.maximum(jnp.max(s, axis=0, keepdims=True), sink)
            e = jnp.exp(s - m)
            den = jnp.sum(e, axis=0, keepdims=True) + jnp.exp(sink - m)
            o_t = _dot(vt_g, e.astype(BF16)) * (1.0 / den)
            a_ref[r0:r0 + q_blk, hd * HEAD_DIM:(hd + 1) * HEAD_DIM] = o_t.T.astype(BF16)

    def retention_head(c, h):
        lg = _LOG_GAMMA[h]
        rows = slice(c * r_chunk, (c + 1) * r_chunk)
        q = qr_ref[rows, h * RET_DK:(h + 1) * RET_DK]
        k = kr_ref[rows, h * RET_DK:(h + 1) * RET_DK]
        v = vr_ref[rows, h * RET_DV:(h + 1) * RET_DV]
        st = state_ref[h]
        sc = _dot_nt(q, k) * dmat_ref[h]
        cdec = cdec_ref[h]
        o = _dot(sc.astype(BF16), v) + _dot(q, st.astype(BF16)) * jnp.concatenate([cdec] * (RET_DV // RET_DK), axis=1)
        kd = (k.astype(F32) * kdec_ref[h]).astype(BF16)
        state_ref[h] = st * float(np.exp(np.float32(lg) * np.float32(r_chunk))) + _dot_tn(kd, v)
        c_o = o - jnp.mean(o, axis=-1, keepdims=True)
        nrm = c_o * lax.rsqrt(jnp.mean(c_o * c_o, axis=-1, keepdims=True) + EPS)
        r_ref[rows, h * RET_DV:(h + 1) * RET_DV] = nrm.astype(BF16)

    attn_groups = [(qb, g) for qb in range(t_tok // q_blk) for g in range(N_KV_HEADS)]
    ret_heads = [(c, h) for c in range(t_tok // r_chunk) for h in range(N_RET_HEADS)]
    items = []
    for i in range(max(len(attn_groups), len(ret_heads))):
        if i < len(ret_heads):
            items.append(functools.partial(retention_head, *ret_heads[i]))
        if i < len(attn_groups):
            items.append(functools.partial(attention_group, *attn_groups[i]))
    return items


def _seq_scratch(t_tok, q_blk, r_chunk, with_pen):
    shapes = [
        pltpu.VMEM((N_RET_HEADS, RET_DK, RET_DV), F32),
        pltpu.VMEM((SWA_ROWS + t_tok, KV_COLS), BF16),
        pltpu.VMEM((N_Q_HEADS, SWA_ROWS + q_blk, q_blk), F32),
        pltpu.VMEM((N_RET_HEADS, r_chunk, r_chunk), F32),
        pltpu.VMEM((N_RET_HEADS, r_chunk, RET_DK), F32),
        pltpu.VMEM((N_RET_HEADS, r_chunk, RET_DK), F32),
    ]
    if with_pen:
        shapes.append(pltpu.VMEM((2, SWA_ROWS + q_blk, q_blk), F32))
    return shapes


def _seq_mix_kernel(vr_ref, qa_ref, qr_ref, kr_ref, kv_ref, ck_ref, cv_ref, st_in_ref, sink_ref,
                    a_ref, r_ref, st_out_ref,
                    state_ref, kf_ref, bias_ref, dmat_ref, cdec_ref, kdec_ref, *, t_tok):
    @pl.when(pl.program_id(0) == 0)
    def _():
        _fill_tables(bias_ref, None, dmat_ref, cdec_ref, kdec_ref, q_blk=t_tok, r_chunk=t_tok)

    state_ref[...] = st_in_ref[...]
    kf_ref[0:SWA_ROWS, 0:KV_A_DIM] = ck_ref[...].astype(BF16)
    kf_ref[0:SWA_ROWS, KV_A_DIM:KV_COLS] = cv_ref[...].astype(BF16)
    kf_ref[SWA_ROWS:SWA_ROWS + t_tok, :] = kv_ref[...]
    for item in _seq_mix_items(
            vr_ref=vr_ref, qa_ref=qa_ref, qr_ref=qr_ref, kr_ref=kr_ref, kf_ref=kf_ref,
            sink_ref=sink_ref, a_ref=a_ref, r_ref=r_ref, state_ref=state_ref, bias_ref=bias_ref, pen_ref=None,
            dmat_ref=dmat_ref, cdec_ref=cdec_ref, kdec_ref=kdec_ref, first_tile=None,
            t_tok=t_tok, q_blk=t_tok, r_chunk=t_tok):
        item()
    st_out_ref[...] = state_ref[...]


def _seq_mix(proj_a, proj_b, sinks, cache, *, n_b, t_tok):
    ck, cv, st_in = cache
    m = n_b * t_tok
    return pl.pallas_call(
        functools.partial(_seq_mix_kernel, t_tok=t_tok),
        grid=(n_b,),
        in_specs=[
            pl.BlockSpec((t_tok, RET_V_DIM), lambda b: (b, A_VR)),
            pl.BlockSpec((t_tok, Q_A_DIM), lambda b: (b, B_QA)),
            pl.BlockSpec((t_tok, RET_QK_DIM), lambda b: (b, B_QR)),
            pl.BlockSpec((t_tok, RET_QK_DIM), lambda b: (b, B_KR)),
            pl.BlockSpec((t_tok, KV_COLS), lambda b: (b, B_KV)),
            pl.BlockSpec((None, SWA_ROWS, KV_A_DIM), lambda b: (b, 0, 0)),
            pl.BlockSpec((None, SWA_ROWS, KV_A_DIM), lambda b: (b, 0, 0)),
            pl.BlockSpec((None, N_RET_HEADS, RET_DK, RET_DV), lambda b: (b, 0, 0, 0)),
            pl.BlockSpec(memory_space=pltpu.SMEM),
        ],
        out_specs=[
            pl.BlockSpec((t_tok, Q_A_DIM), lambda b: (b, 0)),
            pl.BlockSpec((t_tok, RET_V_DIM), lambda b: (b, 0)),
            pl.BlockSpec((None, N_RET_HEADS, RET_DK, RET_DV), lambda b: (b, 0, 0, 0)),
        ],
        out_shape=[
            jax.ShapeDtypeStruct((m, Q_A_DIM), BF16),
            jax.ShapeDtypeStruct((m, RET_V_DIM), BF16),
            jax.ShapeDtypeStruct((n_b, N_RET_HEADS, RET_DK, RET_DV), F32),
        ],
        scratch_shapes=_seq_scratch(t_tok, t_tok, t_tok, with_pen=False),
        compiler_params=_params(1),
        name="seq_mix_sample",
    )(proj_a, proj_b, proj_b, proj_b, proj_b, ck, cv, st_in, sinks)


def _merge_math(x, a, r_normed, gr, ga, gb, wau_ref, wru_ref, wout_ref, gpost_ref):
    gr = gr.astype(F32)
    r = (r_normed.astype(F32) * (gr * _sigmoid(gr))).astype(BF16)
    merged = (_sigmoid(ga.astype(F32)) * _dot(a, wau_ref[...])
              + _sigmoid(gb.astype(F32)) * _dot(r, wru_ref[...]))
    mo = _dot(merged.astype(BF16), wout_ref[...])
    return x + _rms(mo, gpost_ref[...])


def _merge_kernel(x_ref, a_ref, r_ref, gr_ref, ga_ref, gb_ref, wau_ref, wru_ref, wout_ref, gpost_ref, h_ref):
    h_ref[...] = _merge_math(x_ref[...], a_ref[...], r_ref[...], gr_ref[...], ga_ref[...], gb_ref[...],
                             wau_ref, wru_ref, wout_ref, gpost_ref)


def _merge(x, a, r, proj_a, proj_b, wau, wru, wout, gpost, *, tm):
    m = x.shape[0]
    assert m % tm == 0
    const = lambda shape: pl.BlockSpec(shape, lambda i: (0,) * len(shape), pipeline_mode=pl.Buffered(1))
    return pl.pallas_call(
        _merge_kernel,
        grid=(m // tm,),
        in_specs=[
            pl.BlockSpec((tm, D_MODEL), lambda i: (i, 0)),
            pl.BlockSpec((tm, Q_A_DIM), lambda i: (i, 0)),
            pl.BlockSpec((tm, RET_V_DIM), lambda i: (i, 0)),
            pl.BlockSpec((tm, RET_V_DIM), lambda i: (i, A_GR)),
            pl.BlockSpec((tm, D_MODEL), lambda i: (i, A_GA)),
            pl.BlockSpec((tm, D_MODEL), lambda i: (i, B_GB)),
            const((Q_A_DIM, D_MODEL)),
            const((RET_V_DIM, D_MODEL)),
            const((D_MODEL, D_MODEL)),
            const((1, D_MODEL)),
        ],
        out_specs=pl.BlockSpec((tm, D_MODEL), lambda i: (i, 0)),
        out_shape=jax.ShapeDtypeStruct((m, D_MODEL), F32),
        compiler_params=_params(1),
        name="merge",
    )(x, a, r, proj_a, proj_a, proj_b, wau, wru, wout, gpost)


def _mixer_kernel(vr_ref, qa_ref, qr_ref, kr_ref, kv_ref, kvh_ref, sink_ref,
                  x_ref, gr_ref, ga_ref, gb_ref, wau_ref, wru_ref, wout_ref, gpost_ref,
                  h_ref, st_out_ref,
                  state_ref, kf_ref, bias_ref, dmat_ref, cdec_ref, kdec_ref, pen_ref, a_scr, r_scr,
                  *, t_tok, q_blk, r_chunk, n_t, n_tiles):
    s = pl.program_id(0)
    t = lax.rem(jnp.minimum(s, n_tiles - 1), n_t)

    @pl.when(s == 0)
    def _():
        _fill_tables(bias_ref, pen_ref, dmat_ref, cdec_ref, kdec_ref, q_blk=q_blk, r_chunk=r_chunk)
        a_scr[1] = jnp.zeros(a_scr.shape[1:], BF16)
        r_scr[1] = jnp.zeros(r_scr.shape[1:], BF16)

    @pl.when(t == 0)
    def _():
        state_ref[...] = jnp.zeros_like(state_ref)

    kf_ref[0:SWA_ROWS, :] = kvh_ref[...]
    kf_ref[SWA_ROWS:SWA_ROWS + t_tok, :] = kv_ref[...]
    first_tile = jnp.where(t == 0, 1, 0)

    def step(w, r):
        h_ref[...] = _merge_math(x_ref[...], a_scr[r], r_scr[r], gr_ref[...], ga_ref[...], gb_ref[...],
                                 wau_ref, wru_ref, wout_ref, gpost_ref)
        for item in _seq_mix_items(
                vr_ref=vr_ref, qa_ref=qa_ref, qr_ref=qr_ref, kr_ref=kr_ref, kf_ref=kf_ref,
                sink_ref=sink_ref, a_ref=a_scr.at[w], r_ref=r_scr.at[w], state_ref=state_ref, bias_ref=bias_ref,
                pen_ref=pen_ref, dmat_ref=dmat_ref, cdec_ref=cdec_ref, kdec_ref=kdec_ref, first_tile=first_tile,
                t_tok=t_tok, q_blk=q_blk, r_chunk=r_chunk):
            item()

    w = lax.rem(s, 2)
    step(w, 1 - w)

    @pl.when((t == n_t - 1) & (s < n_tiles))
    def _():
        st_out_ref[...] = state_ref[...]


def _mixer(x, proj_a, proj_b, sinks, wau, wru, wout, gpost, *, n_b, n_t, t_tok, q_blk, r_chunk):
    assert t_tok % q_blk == 0 and t_tok % r_chunk == 0 and q_blk == SWA_ROWS
    n_tiles = n_b * n_t
    halo_per_tile = t_tok // SWA_ROWS
    seq_tile = lambda s: jnp.minimum(s, n_tiles - 1)
    mrg_tile = lambda s: jnp.maximum(s - 1, 0)
    const = lambda shape: pl.BlockSpec(shape, lambda s: (0,) * len(shape), pipeline_mode=pl.Buffered(1))
    return pl.pallas_call(
        functools.partial(_mixer_kernel, t_tok=t_tok, q_blk=q_blk, r_chunk=r_chunk, n_t=n_t, n_tiles=n_tiles),
        grid=(n_tiles + 1,),
        in_specs=[
            pl.BlockSpec((t_tok, RET_V_DIM), lambda s: (seq_tile(s), A_VR)),
            pl.BlockSpec((t_tok, Q_A_DIM), lambda s: (seq_tile(s), B_QA)),
            pl.BlockSpec((t_tok, RET_QK_DIM), lambda s: (seq_tile(s), B_QR)),
            pl.BlockSpec((t_tok, RET_QK_DIM), lambda s: (seq_tile(s), B_KR)),
            pl.BlockSpec((t_tok, KV_COLS), lambda s: (seq_tile(s), B_KV)),
            pl.BlockSpec((SWA_ROWS, KV_COLS), lambda s: (jnp.maximum(seq_tile(s) * halo_per_tile - 1, 0), B_KV)),
            pl.BlockSpec(memory_space=pltpu.SMEM),
            pl.BlockSpec((t_tok, D_MODEL), lambda s: (mrg_tile(s), 0)),
            pl.BlockSpec((t_tok, RET_V_DIM), lambda s: (mrg_tile(s), A_GR)),
            pl.BlockSpec((t_tok, D_MODEL), lambda s: (mrg_tile(s), A_GA)),
            pl.BlockSpec((t_tok, D_MODEL), lambda s: (mrg_tile(s), B_GB)),
            const((Q_A_DIM, D_MODEL)),
            const((RET_V_DIM, D_MODEL)),
            const((D_MODEL, D_MODEL)),
            const((1, D_MODEL)),
        ],
        out_specs=[
            pl.BlockSpec((t_tok, D_MODEL), lambda s: (mrg_tile(s), 0)),
            pl.BlockSpec((None, N_RET_HEADS, RET_DK, RET_DV), lambda s: (seq_tile(s) // n_t, 0, 0, 0)),
        ],
        out_shape=[
            jax.ShapeDtypeStruct((n_tiles * t_tok, D_MODEL), F32),
            jax.ShapeDtypeStruct((n_b, N_RET_HEADS, RET_DK, RET_DV), F32),
        ],
        scratch_shapes=_seq_scratch(t_tok, q_blk, r_chunk, with_pen=True) + [
            pltpu.VMEM((2, t_tok, Q_A_DIM), BF16),
            pltpu.VMEM((2, t_tok, RET_V_DIM), BF16),
        ],
        compiler_params=_params(1),
        name="mixer_prompt",
    )(proj_a, proj_b, proj_b, proj_b, proj_b, proj_b, sinks, x, proj_a, proj_a, proj_b, wau, wru, wout, gpost)


def _ffn_kernel(h_ref, wg_ref, wu_ref, wo_ref, gpre_ref, gpost_ref, y_ref, t_ref, *, n_f):
    j = pl.program_id(1)
    tm = t_ref.shape[0]

    def partial_out(rows):
        tt = t_ref[rows, :]
        g = _dot(tt, wg_ref[...])
        up = _dot(tt, wu_ref[...])
        return _dot((g * _sigmoid(g) * up).astype(BF16), wo_ref[...])

    chunks = [slice(c * EDGE_ROWS, (c + 1) * EDGE_ROWS) for c in range(tm // EDGE_ROWS)]

    @pl.when(j == 0)
    def _():
        for rows in chunks:
            t_ref[rows, :] = _rms(h_ref[rows, :], gpre_ref[...]).astype(BF16)
            y_ref[rows, :] = partial_out(rows)

    @pl.when((j > 0) & (j < n_f - 1))
    def _():
        y_ref[...] += partial_out(slice(None))

    @pl.when(j == n_f - 1)
    def _():
        for rows in chunks:
            y_ref[rows, :] = h_ref[rows, :] + _rms(y_ref[rows, :] + partial_out(rows), gpost_ref[...])


def _ffn(h, wff_in, wff_out, gpre, gpost, *, tm, tf):
    m = h.shape[0]
    n_f = D_FF // tf
    assert m % tm == 0 and D_FF % tf == 0 and n_f >= 2 and tm % EDGE_ROWS == 0
    const = lambda shape: pl.BlockSpec(shape, lambda i, j: (0,) * len(shape), pipeline_mode=pl.Buffered(1))
    return pl.pallas_call(
        functools.partial(_ffn_kernel, n_f=n_f),
        grid=(m // tm, n_f),
        in_specs=[
            pl.BlockSpec((tm, D_MODEL), lambda i, j: (i, 0)),
            pl.BlockSpec((D_MODEL, tf), lambda i, j: (0, j)),
            pl.BlockSpec((D_MODEL, tf), lambda i, j: (0, j + n_f)),
            pl.BlockSpec((tf, D_MODEL), lambda i, j: (j, 0)),
            const((1, D_MODEL)),
            const((1, D_MODEL)),
        ],
        out_specs=pl.BlockSpec((tm, D_MODEL), lambda i, j: (i, 0)),
        out_shape=jax.ShapeDtypeStruct((m, D_MODEL), F32),
        scratch_shapes=[pltpu.VMEM((tm, D_MODEL), BF16)],
        compiler_params=_params(vmem_limit_bytes=V7X_VMEM_FFN_LIMIT_BYTES),
        name="ffn",
    )(h, wff_in, wff_in, wff_out, gpre, gpost)


def _ple_kernel(h_ref, p_ref, gple_ref, wpg_ref, wple_ref, y_ref):
    for c in range(h_ref.shape[0] // EDGE_ROWS):
        rows = slice(c * EDGE_ROWS, (c + 1) * EDGE_ROWS)
        h = h_ref[rows, :]
        gate = _sigmoid(_dot(_rms(h, gple_ref[...]).astype(BF16), wpg_ref[...]))
        y_ref[rows, :] = h + gate * _dot(p_ref[rows, :].astype(BF16), wple_ref[...])


def _ple(h, ple, gple, wpg, wple, *, tm):
    m = h.shape[0]
    assert m % tm == 0 and tm % EDGE_ROWS == 0
    const = lambda shape: pl.BlockSpec(shape, lambda i: (0,) * len(shape), pipeline_mode=pl.Buffered(1))
    return pl.pallas_call(
        _ple_kernel,
        grid=(m // tm,),
        in_specs=[
            pl.BlockSpec((tm, D_MODEL), lambda i: (i, 0)),
            pl.BlockSpec((tm, PLE_DIM), lambda i: (i, 0)),
            const((1, D_MODEL)),
            const((D_MODEL, D_MODEL)),
            const((PLE_DIM, D_MODEL)),
        ],
        out_specs=pl.BlockSpec((tm, D_MODEL), lambda i: (i, 0)),
        out_shape=jax.ShapeDtypeStruct((m, D_MODEL), F32),
        compiler_params=_params(1),
        name="ple",
    )(h, ple, gple, wpg, wple)


def _layer(x, ple, w, *, n_b, seq, cache=None):
    is_prompt = cache is None
    tm = 1024 if x.shape[0] % 1024 == 0 else 512
    proj_a, proj_b, kv = _in_proj(x, w["g_pre_mix"], *w["w_in"], tm_a=tm, tm_b=tm, n_tiles_a=3, n_tiles_b=2)
    if is_prompt:
        h, state = _mixer(x, proj_a, proj_b, w["sinks"], w["w_a_up"], w["w_r_up"], w["w_out"], w["g_post_mix"],
                          n_b=n_b, n_t=seq // 256, t_tok=256, q_blk=SWA_ROWS, r_chunk=256)
    else:
        a, r, state = _seq_mix(proj_a, proj_b, w["sinks"], cache, n_b=n_b, t_tok=seq)
        h = _merge(x, a, r, proj_a, proj_b, w["w_a_up"], w["w_r_up"], w["w_out"], w["g_post_mix"], tm=256)
    h = _ffn(h, w["w_ffn_in"], w["w_ffn_out"], w["g_pre_ffn"], w["g_post_ffn"], tm=tm, tf=512)
    y = _ple(h, ple, w["g_ple"], w["w_ple_gate"], w["w_ple"], tm=tm)
    kv = kv.reshape(n_b, seq, KV_COLS)[:, max(seq - SWA_ROWS, 0):]
    return y, kv[:, :, :KV_A_DIM], kv[:, :, KV_A_DIM:], state


def kernel(x_prompt, x_sample, cache_swa_k, cache_swa_v, state_ret, p_prompt, p_sample, norm_pre_mix, w_in, attn_sinks, w_a_up, w_r_up, w_out, norm_post_mix, norm_pre_ffn, w_ffn_in, w_ffn_out, norm_post_ffn, norm_ple, w_ple_gate, w_ple):
    depth = w_in.shape[0]
    n_bp, seq_p, _ = x_prompt.shape
    n_bs, seq_s, _ = x_sample.shape
    hp = x_prompt.reshape(n_bp * seq_p, D_MODEL)
    hs = x_sample.reshape(n_bs * seq_s, D_MODEL)
    at = (lambda a, l: a.reshape(a.shape[1:])) if depth == 1 else (lambda a, l: a[l])
    outs = [[] for _ in range(6)]
    for l in range(depth):
        w = {
            "w_in": _regroup_w_in(at(w_in, l)),
            "w_a_up": at(w_a_up, l).astype(BF16),
            "w_r_up": at(w_r_up, l).astype(BF16),
            "w_out": at(w_out, l).astype(BF16),
            "w_ffn_in": at(w_ffn_in, l).astype(BF16),
            "w_ffn_out": at(w_ffn_out, l).astype(BF16),
            "w_ple_gate": at(w_ple_gate, l).astype(BF16),
            "w_ple": at(w_ple, l).astype(BF16),
            "sinks": at(attn_sinks, l),
            "g_pre_mix": at(norm_pre_mix, l).reshape(1, D_MODEL),
            "g_post_mix": at(norm_post_mix, l).reshape(1, D_MODEL),
            "g_pre_ffn": at(norm_pre_ffn, l).reshape(1, D_MODEL),
            "g_post_ffn": at(norm_post_ffn, l).reshape(1, D_MODEL),
            "g_ple": at(norm_ple, l).reshape(1, D_MODEL),
        }
        hp, kp, vp, sp = _layer(hp, at(p_prompt, l).reshape(n_bp * seq_p, PLE_DIM), w, n_b=n_bp, seq=seq_p)
        cache = (at(cache_swa_k, l).reshape(n_bs, SWA_ROWS, KV_A_DIM), at(cache_swa_v, l).reshape(n_bs, SWA_ROWS, KV_A_DIM),
                 at(state_ret, l))
        hs, k_s, v_s, s_s = _layer(hs, at(p_sample, l).reshape(n_bs * seq_s, PLE_DIM), w, n_b=n_bs, seq=seq_s, cache=cache)
        kv_shape = (SWA_ROWS, N_KV_HEADS, HEAD_DIM)
        outs[0].append(kp.reshape(n_bp, *kv_shape))
        outs[1].append(vp.reshape(n_bp, *kv_shape))
        outs[2].append(sp)
        outs[3].append(jnp.concatenate([cache[0][:, seq_s:], k_s], axis=1).reshape(n_bs, *kv_shape))
        outs[4].append(jnp.concatenate([cache[1][:, seq_s:], v_s], axis=1).reshape(n_bs, *kv_shape))
        outs[5].append(s_s)
    stack = (lambda o: o[0][None]) if depth == 1 else jnp.stack
    return (hp.reshape(n_bp, seq_p, D_MODEL), hs.reshape(n_bs, seq_s, D_MODEL), *(stack(o) for o in outs))
```
